```python
import jax, jax.numpy as jnp
from jax import lax
import numpy as np

D_MODEL = 1024
BATCH = 16
SEQ = 2048
DEPTH = 2

CTX_LEN = 256
GRID_W = 64
MIX_WIDTH = D_MODEL
GROUP_W = MIX_WIDTH // 4
HEAD_DIM = 64
CONV_CH = GROUP_W
CONV_K = 31
FNET_GROUPS = GROUP_W // HEAD_DIM
NAT_HEADS = GROUP_W // HEAD_DIM
NAT_MAX_ROWS = 8
NAT_COLS = 16
NAT_QC = 16
NAT_KC = NAT_QC + NAT_COLS
SWA_Q_HEADS = GROUP_W // HEAD_DIM
SWA_KV_HEADS = 2
SWA_WINDOW = 128
SWA_BLOCK = 128
ROPE_BASE = 10000.0
FFN_DIM = 2816
MACARON_W = 0.5
N_MOD = 9
EPS = 1e-6
NEG = -1e30
IN_WIDTHS = (2 * CONV_CH, GROUP_W,
             NAT_HEADS * HEAD_DIM, NAT_HEADS * HEAD_DIM, NAT_HEADS * HEAD_DIM,
             SWA_Q_HEADS * HEAD_DIM, SWA_KV_HEADS * HEAD_DIM, SWA_KV_HEADS * HEAD_DIM)
IN_DIM = sum(IN_WIDTHS)
IN_SPLITS = tuple(int(v) for v in np.cumsum(IN_WIDTHS)[:-1])

kernel_name = "hybrid_parallel_group_dit_block"


def rms_norm(x, g):
    x32 = x.astype(jnp.float32)
    y = x32 * lax.rsqrt(jnp.mean(x32 * x32, axis=-1, keepdims=True) + EPS)
    return (y * g.astype(jnp.float32)).astype(x.dtype)


def modulate(x, shift, scale):
    return x * (1 + scale) + shift


def ffn_sublayer(x, shift, scale, gate, g_pre, g_post, w1, w3, w2):
    h = modulate(rms_norm(x, g_pre), shift, scale)
    y = (jax.nn.silu(h @ w1) * (h @ w3)) @ w2
    return x + MACARON_W * gate * rms_norm(y, g_post)


def split_heads(t, nh):
    return t.reshape(t.shape[0], t.shape[1], nh, t.shape[-1] // nh)


def conformer_conv(u, w, b, ln_g, ln_b):
    a = u[..., :CONV_CH] * jax.nn.sigmoid(u[..., CONV_CH:])
    y = lax.conv_general_dilated(a, w[:, None, :], window_strides=(1,),
                                 padding=[(CONV_K // 2, CONV_K // 2)],
                                 dimension_numbers=('NWC', 'WIO', 'NWC'),
                                 feature_group_count=CONV_CH) + b
    y32 = y.astype(jnp.float32)
    mu = jnp.mean(y32, axis=-1, keepdims=True)
    var = jnp.mean(jnp.square(y32 - mu), axis=-1, keepdims=True)
    yn = (y32 - mu) * lax.rsqrt(var + EPS) * ln_g.astype(jnp.float32) + ln_b.astype(jnp.float32)
    return jax.nn.silu(yn).astype(u.dtype)


def fourier_mix(u):
    b, n, _ = u.shape
    z = u.astype(jnp.float32).reshape(b, n, FNET_GROUPS, -1)
    y = jnp.fft.fft2(z, axes=(1, 3), norm='ortho').real
    return y.reshape(b, n, -1).astype(u.dtype)


def rope_1d(x, pos):
    half = x.shape[-1] // 2
    inv = ROPE_BASE ** (-jnp.arange(half, dtype=jnp.float32) / half)
    ang = pos[:, None] * inv[None, :]
    cos = jnp.cos(ang)[:, None, :]
    sin = jnp.sin(ang)[:, None, :]
    x1, x2 = x[..., :half], x[..., half:]
    return jnp.concatenate([x1 * cos - x2 * sin, x1 * sin + x2 * cos], axis=-1)


def axial_rope(x, rows, cols):
    x32 = x.astype(jnp.float32)
    h = x.shape[-1] // 2
    return jnp.concatenate([rope_1d(x32[..., :h], rows), rope_1d(x32[..., h:], cols)], axis=-1).astype(x.dtype)


def nat_geometry(rows):
    wr = min(NAT_MAX_ROWS, rows)
    ncb = GRID_W // NAT_QC
    r = np.arange(rows)
    rs = np.clip(r - wr // 2, 0, rows - wr)
    kr = rs[:, None] + np.arange(wr)[None, :]
    cb = np.clip(np.arange(ncb) * NAT_QC - NAT_COLS // 2, 0, GRID_W - NAT_KC)
    kc = cb[:, None] + np.arange(NAT_KC)[None, :]
    qc = np.arange(ncb)[:, None] * NAT_QC + np.arange(NAT_QC)[None, :]
    ws = np.clip(qc - NAT_COLS // 2, 0, GRID_W - NAT_COLS)
    col_ok = (kc[:, None, :] >= ws[..., None]) & (kc[:, None, :] < ws[..., None] + NAT_COLS)
    idx = kr[:, None, :, None] * GRID_W + kc[None, :, None, :]
    dr = kr - r[:, None] + NAT_MAX_ROWS - 1
    dc = np.clip(kc[:, None, :] - qc[..., None] + NAT_COLS - 1, 0, 2 * NAT_COLS - 2)
    return wr, idx, col_ok, dr, dc


def nat_attention(q, k, v, kx, vx, rel_bias):
    b, n, h, d = q.shape
    rows = n // GRID_W
    ncb = GRID_W // NAT_QC
    wr, idx, col_ok, dr, dc = nat_geometry(rows)
    nj = wr * NAT_KC
    qb = q.reshape(b, rows, ncb, NAT_QC, h, d)
    flat = jnp.asarray(idx.reshape(-1))
    kb = jnp.take(k, flat, axis=1).reshape(b, rows, ncb, nj, h, d)
    vb = jnp.take(v, flat, axis=1).reshape(b, rows, ncb, nj, h, d)
    bias = rel_bias.astype(jnp.float32)[:, dr[:, None, None, :, None], dc[None, :, :, None, :]]
    bias = jnp.where(col_ok[None, None, :, :, None, :], bias, NEG)
    bias = bias.reshape(h, rows, ncb, NAT_QC, nj).transpose(1, 2, 0, 3, 4)
    scale = d ** -0.5
    s_loc = jnp.einsum('brcqhd,brcjhd->brchqj', qb, kb).astype(jnp.float32) * scale + bias
    s_ctx = jnp.einsum('brcqhd,blhd->brchql', qb, kx).astype(jnp.float32) * scale
    p = jax.nn.softmax(jnp.concatenate([s_loc, s_ctx], axis=-1), axis=-1).astype(v.dtype)
    o = (jnp.einsum('brchqj,brcjhd->brcqhd', p[..., :nj], vb)
         + jnp.einsum('brchql,blhd->brcqhd', p[..., nj:], vx))
    return o.reshape(b, n, h * d)


def band_attention(q, k, v, kx, vx, sink):
    b, n, hq, d = q.shape
    hkv = k.shape[2]
    g = hq // hkv
    nb = n // SWA_BLOCK
    nl = kx.shape[1]
    qb = q.reshape(b, nb, SWA_BLOCK, hkv, g, d)

    def band(t):
        tp = jnp.pad(t, ((0, 0), (SWA_BLOCK, SWA_BLOCK), (0, 0), (0, 0))).reshape(b, nb + 2, SWA_BLOCK, hkv, d)
        return jnp.concatenate([tp[:, :-2], tp[:, 1:-1], tp[:, 2:]], axis=2)

    kb, vb = band(k), band(v)
    nj = 3 * SWA_BLOCK
    rel = np.arange(nj)[None, :] - SWA_BLOCK - np.arange(SWA_BLOCK)[:, None]
    kpos = (np.arange(nb)[:, None] - 1) * SWA_BLOCK + np.arange(nj)[None, :]
    ok = (np.abs(rel) <= SWA_WINDOW)[None] & ((kpos >= 0) & (kpos < n))[:, None, :]
    scale = d ** -0.5
    s_loc = jnp.einsum('bnqkgd,bnjkd->bnkgqj', qb, kb).astype(jnp.float32) * scale
    s_loc = jnp.where(ok[None, :, None, None], s_loc, NEG)
    s_ctx = jnp.einsum('bnqkgd,blkd->bnkgql', qb, kx).astype(jnp.float32) * scale
    s_sink = jnp.broadcast_to(sink.astype(jnp.float32).reshape(hkv, g)[None, None, :, :, None, None],
                              s_loc.shape[:-1] + (1,))
    p = jax.nn.softmax(jnp.concatenate([s_loc, s_ctx, s_sink], axis=-1), axis=-1).astype(v.dtype)
    o = (jnp.einsum('bnkgqj,bnjkd->bnqkgd', p[..., :nj], vb)
         + jnp.einsum('bnkgql,blkd->bnqkgd', p[..., nj:nj + nl], vx))
    return o.reshape(b, n, hq * d)


def dense_ctx_attention(q, k, v, sink):
    bsz, nl, nk, g, d = q.shape
    s = jnp.einsum('blkgd,bmkd->bkglm', q, k).astype(jnp.float32) * d ** -0.5
    if sink is None:
        p = jax.nn.softmax(s, axis=-1)
    else:
        s_sink = jnp.broadcast_to(sink.astype(jnp.float32).reshape(nk, g)[None, :, :, None, None], s.shape[:-1] + (1,))
        p = jax.nn.softmax(jnp.concatenate([s, s_sink], axis=-1), axis=-1)[..., :-1]
    o = jnp.einsum('bkglm,bmkd->blkgd', p.astype(v.dtype), v)
    return o.reshape(bsz, nl, nk * g * d)


def token_mix(h, hc, w_in, conv_w, conv_b, conv_ln_g, conv_ln_b, nat_bias, sinks, w_out, ctx_out):
    n = h.shape[1]
    t = jnp.arange(n, dtype=jnp.int32)
    rows = (t // GRID_W).astype(jnp.float32)
    cols = (t % GRID_W).astype(jnp.float32)
    ua, ub, qn, kn, vn, qs, ks, vs = jnp.split(h @ w_in, IN_SPLITS, axis=-1)
    xa, xb, cqn, ckn, cvn, cqs, cks, cvs = jnp.split(hc @ w_in, IN_SPLITS, axis=-1)
    ckn, cvn = split_heads(ckn, NAT_HEADS), split_heads(cvn, NAT_HEADS)
    cks, cvs = split_heads(cks, SWA_KV_HEADS), split_heads(cvs, SWA_KV_HEADS)
    y_a = conformer_conv(ua, conv_w, conv_b, conv_ln_g, conv_ln_b)
    y_b = fourier_mix(ub)
    y_c = nat_attention(split_heads(qn, NAT_HEADS), split_heads(kn, NAT_HEADS), split_heads(vn, NAT_HEADS),
                        ckn, cvn, nat_bias)
    y_d = band_attention(axial_rope(split_heads(qs, SWA_Q_HEADS), rows, cols),
                         axial_rope(split_heads(ks, SWA_KV_HEADS), rows, cols),
                         split_heads(vs, SWA_KV_HEADS), cks, cvs, sinks)
    y = jnp.concatenate([y_a, y_b, y_c, y_d], axis=-1) @ w_out
    if not ctx_out:
        return y, None
    bsz, nl = hc.shape[0], hc.shape[1]
    yc_a = conformer_conv(xa, conv_w, conv_b, conv_ln_g, conv_ln_b)
    yc_b = fourier_mix(xb)
    yc_c = dense_ctx_attention(cqn.reshape(bsz, nl, NAT_HEADS, 1, HEAD_DIM), ckn, cvn, None)
    yc_d = dense_ctx_attention(cqs.reshape(bsz, nl, SWA_KV_HEADS, SWA_Q_HEADS // SWA_KV_HEADS, HEAD_DIM),
                               cks, cvs, sinks)
    yc = jnp.concatenate([yc_a, yc_b, yc_c, yc_d], axis=-1) @ w_out
    return y, yc


def setup_inputs(seed: int = 0) -> dict:
    key = jax.random.key(seed)
    ks = jax.random.split(key, 20)
    nrm = jax.random.normal
    f32 = jnp.float32
    d = D_MODEL
    return {
        'x': nrm(ks[0], (BATCH, SEQ, d), f32),
        'c': nrm(ks[1], (BATCH, d), f32),
        'ctx': nrm(ks[2], (BATCH, CTX_LEN, d), f32),
        'c_ctx': nrm(ks[3], (d,), f32),
        'w_ada': nrm(ks[4], (DEPTH, d, N_MOD * d), f32) * (0.5 * d ** -0.5),
        'b_ada': nrm(ks[5], (DEPTH, N_MOD * d), f32) * 0.01,
        'norm_g': 1.0 + 0.05 * nrm(ks[6], (DEPTH, 6, d), f32),
        'ffn_w1': nrm(ks[7], (DEPTH, 2, d, FFN_DIM), f32) * d ** -0.5,
        'ffn_w3': nrm(ks[8], (DEPTH, 2, d, FFN_DIM), f32) * d ** -0.5,
        'ffn_w2': nrm(ks[9], (DEPTH, 2, FFN_DIM, d), f32) * FFN_DIM ** -0.5,
        'w_in': nrm(ks[10], (DEPTH, d, IN_DIM), f32) * d ** -0.5,
        'conv_w': nrm(ks[11], (DEPTH, CONV_K, CONV_CH), f32) * CONV_K ** -0.5,
        'conv_b': nrm(ks[12], (DEPTH, CONV_CH), f32) * 0.02,
        'conv_ln_g': 1.0 + 0.05 * nrm(ks[13], (DEPTH, CONV_CH), f32),
        'conv_ln_b': nrm(ks[14], (DEPTH, CONV_CH), f32) * 0.02,
        'nat_rel_bias': nrm(ks[15], (DEPTH, NAT_HEADS, 2 * NAT_MAX_ROWS - 1, 2 * NAT_COLS - 1), f32) * 0.2,
        'sink_logits': nrm(ks[16], (DEPTH, SWA_Q_HEADS), f32) * 0.5,
        'w_out': nrm(ks[17], (DEPTH, MIX_WIDTH, d), f32) * MIX_WIDTH ** -0.5,
    }


def reference(x, c, ctx, c_ctx, w_ada, b_ada, norm_g, ffn_w1, ffn_w3, ffn_w2, w_in, conv_w, conv_b,
              conv_ln_g, conv_ln_b, nat_rel_bias, sink_logits, w_out):
    bsz, d = x.shape[0], x.shape[-1]
    xl, xc = x, ctx
    sc = jax.nn.silu(c)
    scc = jax.nn.silu(c_ctx)
    for l in range(DEPTH):
        last = l == DEPTH - 1
        mod_l = (sc @ w_ada[l] + b_ada[l]).reshape(bsz, 1, N_MOD, d)
        mod_c = (scc @ w_ada[l] + b_ada[l]).reshape(N_MOD, d)
        xl = ffn_sublayer(xl, mod_l[..., 0, :], mod_l[..., 1, :], mod_l[..., 2, :], norm_g[l, 0], norm_g[l, 1],
                          ffn_w1[l, 0], ffn_w3[l, 0], ffn_w2[l, 0])
        xc = ffn_sublayer(xc, mod_c[0], mod_c[1], mod_c[2], norm_g[l, 0], norm_g[l, 1],
                          ffn_w1[l, 0], ffn_w3[l, 0], ffn_w2[l, 0])
        hl = modulate(rms_norm(xl, norm_g[l, 2]), mod_l[..., 3, :], mod_l[..., 4, :])
        hc = modulate(rms_norm(xc, norm_g[l, 2]), mod_c[3], mod_c[4])
        yl, yc = token_mix(hl, hc, w_in[l], conv_w[l], conv_b[l], conv_ln_g[l], conv_ln_b[l],
                           nat_rel_bias[l], sink_logits[l], w_out[l], not last)
        xl = xl + mod_l[..., 5, :] * rms_norm(yl, norm_g[l, 3])
        xl = ffn_sublayer(xl, mod_l[..., 6, :], mod_l[..., 7, :], mod_l[..., 8, :], norm_g[l, 4], norm_g[l, 5],
                          ffn_w1[l, 1], ffn_w3[l, 1], ffn_w2[l, 1])
        if not last:
            xc = xc + mod_c[5] * rms_norm(yc, norm_g[l, 3])
            xc = ffn_sublayer(xc, mod_c[6], mod_c[7], mod_c[8], norm_g[l, 4], norm_g[l, 5],
                              ffn_w1[l, 1], ffn_w3[l, 1], ffn_w2[l, 1])
    return xl
```

```python
import functools

import numpy as np
import jax
import jax.numpy as jnp
from jax import lax
from jax.experimental import pallas as pl
from jax.experimental.pallas import tpu as pltpu

F32 = jnp.float32
BF16 = jnp.bfloat16

D_MODEL = 1024
SEQ = 2048
CTX_LEN = 256
GRID_W = 64
GRID_H = SEQ // GRID_W
GROUP_W = 256
HEAD_DIM = 64
N_HEADS = 4
CONV_K = 31
CONV_HALO = 16
NAT_ROWS = 8
NAT_COLS = 16
SWA_KV_HEADS = 2
SWA_WINDOW = 128
SWA_BLOCK = 128
ROPE_BASE = 10000.0
FFN_DIM = 2816
FFN_CHUNKS = 2
MACARON_W = 0.5
N_MOD = 9
EPS = 1e-6
NEG = -1e30
Q_SCALE = HEAD_DIM ** -0.5

COL_UA = 0
COL_UB = 512
COL_QN, COL_KN, COL_VN = 768, 1024, 1280
COL_QS, COL_KS, COL_VS = 1536, 1792, 2048
COL_QS_SW, COL_KS_SW = 2304, 2560
N_EXT_CTX = 2304
N_EXT = 2816

V7X_VMEM_LIMIT = 56 * 1024 * 1024


def _resident(shape):
    nd = len(shape)
    return pl.BlockSpec(shape, lambda *_: (0,) * nd, pipeline_mode=pl.Buffered(1))


def _params(n_axes=1, vmem=V7X_VMEM_LIMIT):
    return pltpu.CompilerParams(dimension_semantics=("arbitrary",) * n_axes, vmem_limit_bytes=vmem)


def _rms(x, g):
    return x * lax.rsqrt(jnp.mean(x * x, axis=-1, keepdims=True) + EPS) * g


def _sigmoid(x):
    return 1.0 / (1.0 + jnp.exp(-x))


def _ffn(x, shift, scale, gate, g_pre, g_post, w1_ref, w3_ref, w2_ref):
    h = (_rms(x, g_pre) * (1.0 + scale) + shift).astype(BF16)
    fc = FFN_DIM // FFN_CHUNKS
    y = None
    for c in range(FFN_CHUNKS):
        a = jnp.dot(h, w1_ref[:, c * fc:(c + 1) * fc], preferred_element_type=F32)
        b = jnp.dot(h, w3_ref[:, c * fc:(c + 1) * fc], preferred_element_type=F32)
        t = (a * _sigmoid(a) * b).astype(BF16)
        yc = jnp.dot(t, w2_ref[c * fc:(c + 1) * fc, :], preferred_element_type=F32)
        y = yc if y is None else y + yc
    return x + (MACARON_W * gate) * _rms(y, g_post)


def _ada_kernel(cv_ref, w_ref, b_ref, o_ref):
    c = cv_ref[...]
    sc = (c * _sigmoid(c)).astype(BF16)
    o_ref[...] = jnp.dot(sc, w_ref[...].astype(BF16), preferred_element_type=F32) + b_ref[...]


def _ada_mod(cv, w_ada, b_ada):
    depth, d, n = w_ada.shape
    rows = cv.shape[0]
    tn = 1152
    return pl.pallas_call(
        _ada_kernel,
        grid=(depth, n // tn),
        in_specs=[pl.BlockSpec((rows, d), lambda l, j: (0, 0)),
                  pl.BlockSpec((None, d, tn), lambda l, j: (l, 0, j)),
                  pl.BlockSpec((None, 1, tn), lambda l, j: (l, 0, j))],
        out_specs=pl.BlockSpec((None, rows, tn), lambda l, j: (l, 0, j)),
        out_shape=jax.ShapeDtypeStruct((depth, rows, n), F32),
        compiler_params=_params(2),
        name="ada_mod",
    )(cv, w_ada, b_ada.reshape(depth, 1, n))


def _front_kernel(*refs, rope):
    if rope:
        (x_ref, mod_ref, g_ref, w1_ref, w3_ref, w2_ref, wx_ref, cos_ref, sin_ref,
         x1_ref, ua_ref, ub_ref, qn_ref, kn_ref, vn_ref, qs_ref, ks_ref, vs_ref) = refs
    else:
        (x_ref, mod_ref, g_ref, w1_ref, w3_ref, w2_ref, wx_ref,
         x1_ref, ua_ref, ub_ref, qn_ref, kn_ref, vn_ref, qs_ref, ks_ref, vs_ref) = refs
    x1 = _ffn(x_ref[...], mod_ref[0:1, :], mod_ref[1:2, :], mod_ref[2:3, :],
              g_ref[0:1, :], g_ref[1:2, :], w1_ref, w3_ref, w2_ref)
    x1_ref[...] = x1
    h = (_rms(x1, g_ref[2:3, :]) * (1.0 + mod_ref[4:5, :]) + mod_ref[3:4, :]).astype(BF16)
    u = jnp.dot(h, wx_ref[...], preferred_element_type=F32)
    ua_ref[...] = u[:, COL_UA:COL_UA + 512]
    ub_ref[...] = u[:, COL_UB:COL_UB + 256].astype(BF16)
    qn_ref[...] = (u[:, COL_QN:COL_QN + 256] * Q_SCALE).astype(BF16)
    kn_ref[...] = u[:, COL_KN:COL_KN + 256].astype(BF16)
    vn_ref[...] = u[:, COL_VN:COL_VN + 256].astype(BF16)
    qs = u[:, COL_QS:COL_QS + 256]
    ks = u[:, COL_KS:COL_KS + 256]
    if rope:
        cos, sin = cos_ref[...], sin_ref[...]
        qs = qs * cos + u[:, COL_QS_SW:COL_QS_SW + 256] * sin
        ks = ks * cos + u[:, COL_KS_SW:COL_KS_SW + 256] * sin
    qs_ref[...] = (qs * Q_SCALE).astype(BF16)
    ks_ref[...] = ks.astype(BF16)
    vs_ref[...] = u[:, COL_VS:COL_VS + 256].astype(BF16)


def _front(x, mod, g, w1, w3, w2, wx, rope_tabs, rows_per_batch, tm):
    rows, d = x.shape
    tiles_per_batch = rows_per_batch // tm
    rope = rope_tabs is not None
    row_tile = lambda w: pl.BlockSpec((tm, w), lambda i: (i, 0))
    in_specs = [row_tile(d),
                pl.BlockSpec((None, N_MOD, d), lambda i: (i // tiles_per_batch, 0, 0)),
                _resident(g.shape), _resident(w1.shape), _resident(w3.shape), _resident(w2.shape),
                _resident(wx.shape)]
    args = [x, mod, g, w1, w3, w2, wx]
    if rope:
        tab = pl.BlockSpec((tm, 256), lambda i: (i % tiles_per_batch, 0))
        in_specs += [tab, tab]
        args += list(rope_tabs)
    out_shape = ([jax.ShapeDtypeStruct((rows, d), F32), jax.ShapeDtypeStruct((rows, 512), F32)]
                 + [jax.ShapeDtypeStruct((rows, 256), BF16)] * 7)
    out_specs = [row_tile(d), row_tile(512)] + [row_tile(256)] * 7
    return pl.pallas_call(
        functools.partial(_front_kernel, rope=rope),
        grid=(rows // tm,), in_specs=in_specs, out_specs=out_specs, out_shape=out_shape,
        compiler_params=_params(1), name="front_rope" if rope else "front_ctx",
    )(*args)


def _back_kernel(x_ref, ya_ref, yb_ref, yc_ref, yd_ref, mod_ref, g_ref, wo_ref, w1_ref, w3_ref, w2_ref, o_ref):
    y = None
    for j, y_ref in enumerate((ya_ref, yb_ref, yc_ref, yd_ref)):
        yj = jnp.dot(y_ref[...], wo_ref[j * GROUP_W:(j + 1) * GROUP_W, :], preferred_element_type=F32)
        y = yj if y is None else y + yj
    x2 = x_ref[...] + mod_ref[5:6, :] * _rms(y, g_ref[3:4, :])
    o_ref[...] = _ffn(x2, mod_ref[6:7, :], mod_ref[7:8, :], mod_ref[8:9, :],
                      g_ref[4:5, :], g_ref[5:6, :], w1_ref, w3_ref, w2_ref)


def _back(x, ys, mod, g, wo, w1, w3, w2, rows_per_batch, tm):
    rows, d = x.shape
    tiles_per_batch = rows_per_batch // tm
    row_tile = lambda w: pl.BlockSpec((tm, w), lambda i: (i, 0))
    in_specs = ([row_tile(d)] + [row_tile(GROUP_W)] * 4
                + [pl.BlockSpec((None, N_MOD, d), lambda i: (i // tiles_per_batch, 0, 0)),
                   _resident(g.shape), _resident(wo.shape), _resident(w1.shape), _resident(w3.shape),
                   _resident(w2.shape)])
    return pl.pallas_call(
        _back_kernel, grid=(rows // tm,), in_specs=in_specs, out_specs=row_tile(d),
        out_shape=jax.ShapeDtypeStruct((rows, d), F32),
        compiler_params=_params(1), name="back",
    )(x, *ys, mod, g, wo, w1, w3, w2)


def _conv_kernel(prev_ref, cur_ref, next_ref, w_ref, p_ref, o_ref, pad_ref, *, n_chunks, rc, sub):
    c = pl.program_id(1)

    def glu(u):
        return u[:, :GROUP_W] * _sigmoid(u[:, GROUP_W:])

    pad_ref[0:CONV_HALO, :] = jnp.where(c > 0, glu(prev_ref[...]), 0.0)
    pad_ref[CONV_HALO:CONV_HALO + rc, :] = glu(cur_ref[...])
    pad_ref[CONV_HALO + rc:, :] = jnp.where(c < n_chunks - 1, glu(next_ref[...]), 0.0)
    bias, ln_g, ln_b = p_ref[0:1, :], p_ref[1:2, :], p_ref[2:3, :]
    lead = CONV_HALO - CONV_K // 2
    for s in range(rc // sub):
        acc = jnp.zeros((sub, GROUP_W), F32) + bias
        for k in range(CONV_K):
            r0 = s * sub + lead + k
            acc = acc + pad_ref[r0:r0 + sub, :] * w_ref[k:k + 1, :]
        mu = jnp.mean(acc, axis=-1, keepdims=True)
        cen = acc - mu
        var = jnp.mean(cen * cen, axis=-1, keepdims=True)
        yn = cen * lax.rsqrt(var + EPS) * ln_g + ln_b
        o_ref[s * sub:(s + 1) * sub, :] = (yn * _sigmoid(yn)).astype(BF16)


def _conv(ua, conv_w, conv_p, rows_per_batch):
    rows = ua.shape[0]
    batch = rows // rows_per_batch
    rc = 256
    n_chunks = rows_per_batch // rc
    hb = rc // CONV_HALO
    n_halo = rows // CONV_HALO
    cur = pl.BlockSpec((rc, 512), lambda b, c: (b * n_chunks + c, 0))
    prev = pl.BlockSpec((CONV_HALO, 512), lambda b, c: (jnp.maximum((b * n_chunks + c) * hb - 1, 0), 0))
    nxt = pl.BlockSpec((CONV_HALO, 512), lambda b, c: (jnp.minimum((b * n_chunks + c + 1) * hb, n_halo - 1), 0))
    return pl.pallas_call(
        functools.partial(_conv_kernel, n_chunks=n_chunks, rc=rc, sub=64),
        grid=(batch, n_chunks),
        in_specs=[prev, cur, nxt, _resident(conv_w.shape), _resident(conv_p.shape)],
        out_specs=pl.BlockSpec((rc, GROUP_W), lambda b, c: (b * n_chunks + c, 0)),
        out_shape=jax.ShapeDtypeStruct((rows, GROUP_W), BF16),
        scratch_shapes=[pltpu.VMEM((rc + 2 * CONV_HALO, GROUP_W), F32)],
        compiler_params=_params(2), name="conv",
    )(ua, ua, ua, conv_w, conv_p)


def _fourier_kernel(z_ref, wc_ref, m_ref, o_ref, *, n, scale):
    t = jnp.dot(z_ref[...], wc_ref[...], preferred_element_type=F32)
    zz = jnp.concatenate([t[:, :GROUP_W], t[:, GROUP_W:]], axis=0).astype(BF16)
    y = jnp.dot(m_ref[...], zz, preferred_element_type=F32)
    o_ref[...] = (y * scale).astype(BF16)


def _dft_tables(n):
    k = jnp.arange(n, dtype=jnp.int32)
    ang = (2.0 * np.pi / n) * ((k[:, None] * k[None, :]) % n).astype(F32)
    m = jnp.concatenate([jnp.cos(ang), -jnp.sin(ang)], axis=1).astype(BF16)
    c = jnp.arange(HEAD_DIM, dtype=jnp.int32)
    angc = (2.0 * np.pi / HEAD_DIM) * ((c[:, None] * c[None, :]) % HEAD_DIM).astype(F32)
    eye = jnp.eye(GROUP_W // HEAD_DIM, dtype=F32)
    wc = jnp.concatenate([jnp.kron(eye, jnp.cos(angc)), jnp.kron(eye, jnp.sin(angc))], axis=1).astype(BF16)
    return wc, m


def _fourier(ub, wc, m, rows_per_batch):
    rows = ub.shape[0]
    n = rows_per_batch
    blk = pl.BlockSpec((n, GROUP_W), lambda b: (b, 0))
    return pl.pallas_call(
        functools.partial(_fourier_kernel, n=n, scale=float((n * HEAD_DIM) ** -0.5)),
        grid=(rows // n,), in_specs=[blk, _resident(wc.shape), _resident(m.shape)], out_specs=blk,
        out_shape=jax.ShapeDtypeStruct((rows, GROUP_W), BF16),
        compiler_params=_params(1), name="fourier",
    )(ub, wc, m)


def _head_mask(rows_per_head):
    shape = (N_HEADS * rows_per_head, GROUP_W)
    row_h = lax.broadcasted_iota(jnp.int32, shape, 0) // rows_per_head
    lane_h = lax.broadcasted_iota(jnp.int32, shape, 1) // HEAD_DIM
    return (row_h == lane_h).astype(F32)


def _stack_heads(q, mask_bf16):
    return jnp.concatenate([q] * N_HEADS, axis=0) * mask_bf16


def _unstack_heads(o, mask_f32, rows_per_head):
    om = o * mask_f32
    out = om[0:rows_per_head]
    for h in range(1, N_HEADS):
        out = out + om[h * rows_per_head:(h + 1) * rows_per_head]
    return out


def _dot_nt(a, b):
    return lax.dot_general(a, b, (((1,), (1,)), ((), ())), preferred_element_type=F32)


def _ctx_dense(cq_ref, kx, vx, sink_col, oc_ref):
    mask = _head_mask(CTX_LEN)
    qst = _stack_heads(cq_ref[...], mask.astype(BF16))
    s = _dot_nt(qst, kx)
    m = jnp.max(s, axis=-1, keepdims=True)
    if sink_col is not None:
        m = jnp.maximum(m, sink_col)
    p = jnp.exp(s - m)
    den = jnp.sum(p, axis=-1, keepdims=True)
    if sink_col is not None:
        den = den + jnp.exp(sink_col - m)
    o = jnp.dot(p.astype(BF16), vx, preferred_element_type=F32) / den
    oc_ref[...] = _unstack_heads(o, mask, CTX_LEN).astype(BF16)


def _nat_kernel(*refs, ctx_out):
    if ctx_out:
        q_ref, k_ref, v_ref, kx_ref, vx_ref, bias_ref, cq_ref, o_ref, oc_ref = refs
    else:
        q_ref, k_ref, v_ref, kx_ref, vx_ref, bias_ref, o_ref = refs
    kx, vx = kx_ref[...], vx_ref[...]
    mask = _head_mask(GRID_W)
    mask_b = mask.astype(BF16)
    band = NAT_ROWS * GRID_W

    def row_step(r, carry):
        rs = jnp.clip(r - NAT_ROWS // 2, 0, GRID_H - NAT_ROWS)
        q0 = pl.multiple_of(r * GRID_W, GRID_W)
        k0 = pl.multiple_of(rs * GRID_W, GRID_W)
        qst = _stack_heads(q_ref[pl.ds(q0, GRID_W), :], mask_b)
        s = _dot_nt(qst, k_ref[pl.ds(k0, band), :]) + bias_ref[r - rs]
        sc = _dot_nt(qst, kx)
        m = jnp.maximum(jnp.max(s, axis=-1, keepdims=True), jnp.max(sc, axis=-1, keepdims=True))
        p = jnp.exp(s - m)
        pc = jnp.exp(sc - m)
        den = jnp.sum(p, axis=-1, keepdims=True) + jnp.sum(pc, axis=-1, keepdims=True)
        o = (jnp.dot(p.astype(BF16), v_ref[pl.ds(k0, band), :], preferred_element_type=F32)
             + jnp.dot(pc.astype(BF16), vx, preferred_element_type=F32)) / den
        o_ref[pl.ds(q0, GRID_W), :] = _unstack_heads(o, mask, GRID_W).astype(BF16)
        return carry

    lax.fori_loop(0, GRID_H, row_step, 0)
    if ctx_out:
        _ctx_dense(cq_ref, kx, vx, None, oc_ref)


def _nat_bias_table(rel_bias):
    off = np.arange(NAT_ROWS)[:, None]
    j = np.arange(NAT_ROWS)[None, :]
    dr = j - off + NAT_ROWS - 1
    qc = np.arange(GRID_W)[:, None]
    kc = np.arange(GRID_W)[None, :]
    ws = np.clip(qc - NAT_COLS // 2, 0, GRID_W - NAT_COLS)
    col_ok = (kc >= ws) & (kc < ws + NAT_COLS)
    dc = np.clip(kc - qc + NAT_COLS - 1, 0, 2 * NAT_COLS - 2)
    b = rel_bias.astype(F32)[:, dr[:, None, :, None], dc[None, :, None, :]]
    b = jnp.where(col_ok[None, None, :, None, :], b, NEG)
    return b.transpose(1, 0, 2, 3, 4).reshape(NAT_ROWS, N_HEADS * GRID_W, NAT_ROWS * GRID_W)


def _attention_call(kernel, name, lat, ctx, consts, ctx_out):
    batch = lat[0].shape[0] // SEQ
    lat_blk = pl.BlockSpec((SEQ, GROUP_W), lambda b: (b, 0))
    ctx_blk = pl.BlockSpec((CTX_LEN, GROUP_W), lambda b: (b, 0))
    in_specs = [lat_blk] * 3 + [ctx_blk] * 2 + [_resident(c.shape) for c in consts]
    args = list(lat) + [ctx[1], ctx[2]] + list(consts)
    out_shape = [jax.ShapeDtypeStruct((batch * SEQ, GROUP_W), BF16)]
    out_specs = [lat_blk]
    if ctx_out:
        in_specs.append(ctx_blk)
        args.append(ctx[0])
        out_shape.append(jax.ShapeDtypeStruct((batch * CTX_LEN, GROUP_W), BF16))
        out_specs.append(ctx_blk)
    outs = pl.pallas_call(
        functools.partial(kernel, ctx_out=ctx_out), grid=(batch,), in_specs=in_specs,
        out_specs=out_specs, out_shape=out_shape, compiler_params=_params(1), name=name,
    )(*args)
    return outs if ctx_out else (outs[0], None)


def _swa_kernel(*refs, ctx_out):
    if ctx_out:
        q_ref, k_ref, v_ref, kx_ref, vx_ref, sink_ref, csink_ref, cq_ref, o_ref, oc_ref = refs
    else:
        q_ref, k_ref, v_ref, kx_ref, vx_ref, sink_ref, csink_ref, o_ref = refs
    kx, vx = kx_ref[...], vx_ref[...]
    sink = sink_ref[...]
    mask = _head_mask(SWA_BLOCK)
    mask_b = mask.astype(BF16)
    band = 3 * SWA_BLOCK
    shape = (N_HEADS * SWA_BLOCK, band)
    rel = (lax.broadcasted_iota(jnp.int32, shape, 1)
           - lax.broadcasted_iota(jnp.int32, shape, 0) % SWA_BLOCK)

    def block_step(nb, carry):
        q0 = pl.multiple_of(nb * SWA_BLOCK, SWA_BLOCK)
        k0 = pl.multiple_of(jnp.clip(q0 - SWA_BLOCK, 0, SEQ - band), SWA_BLOCK)
        qst = _stack_heads(q_ref[pl.ds(q0, SWA_BLOCK), :], mask_b)
        s = _dot_nt(qst, k_ref[pl.ds(k0, band), :])
        s = jnp.where(jnp.abs(rel + (k0 - q0)) <= SWA_WINDOW, s, NEG)
        sc = _dot_nt(qst, kx)
        m = jnp.maximum(jnp.maximum(jnp.max(s, axis=-1, keepdims=True), jnp.max(sc, axis=-1, keepdims=True)),
                        sink)
        p = jnp.exp(s - m)
        pc = jnp.exp(sc - m)
        den = jnp.sum(p, axis=-1, keepdims=True) + jnp.sum(pc, axis=-1, keepdims=True) + jnp.exp(sink - m)
        o = (jnp.dot(p.astype(BF16), v_ref[pl.ds(k0, band), :], preferred_element_type=F32)
             + jnp.dot(pc.astype(BF16), vx, preferred_element_type=F32)) / den
        o_ref[pl.ds(q0, SWA_BLOCK), :] = _unstack_heads(o, mask, SWA_BLOCK).astype(BF16)
        return carry

    lax.fori_loop(0, SEQ // SWA_BLOCK, block_step, 0)
    if ctx_out:
        _ctx_dense(cq_ref, kx, vx, csink_ref[...], oc_ref)


def _rope_tables():
    t = np.arange(SEQ)
    pos = np.stack([t // GRID_W, t % GRID_W], axis=1).astype(np.float32)
    half = HEAD_DIM // 4
    inv = ROPE_BASE ** (-jnp.arange(half, dtype=F32) / half)
    ang = jnp.asarray(pos)[:, :, None] * inv[None, None, :]
    cos = jnp.cos(ang)
    sin = jnp.sin(ang)
    cos64 = jnp.concatenate([cos, cos], axis=-1).reshape(SEQ, HEAD_DIM)
    sin64 = jnp.concatenate([-sin, sin], axis=-1).reshape(SEQ, HEAD_DIM)
    return jnp.tile(cos64, (1, N_HEADS)), jnp.tile(sin64, (1, N_HEADS))


def _extended_w_in(w_in):
    d64 = np.arange(HEAD_DIM)
    swap64 = np.where(d64 % 32 < 16, d64 + 16, d64 - 16)
    kv_rep = np.concatenate([(h // (N_HEADS // SWA_KV_HEADS)) * HEAD_DIM + d64 for h in range(N_HEADS)])
    q_sw = np.concatenate([h * HEAD_DIM + swap64 for h in range(N_HEADS)])
    o_qs, o_ks, o_vs = 1536, 1792, 1920
    cols = np.concatenate([np.arange(0, 1536),
                           o_qs + np.arange(256), o_ks + kv_rep, o_vs + kv_rep,
                           o_qs + q_sw, o_ks + kv_rep[q_sw]])
    return w_in[:, cols].astype(BF16)


def kernel(x, c, ctx, c_ctx, w_ada, b_ada, norm_g, ffn_w1, ffn_w3, ffn_w2, w_in, conv_w, conv_b, conv_ln_g,
           conv_ln_b, nat_rel_bias, sink_logits, w_out):
    bsz, seq, d = x.shape
    depth = w_ada.shape[0]
    assert (seq, d, ctx.shape[1]) == (SEQ, D_MODEL, CTX_LEN)
    xl = x.reshape(bsz * seq, d)
    xc = ctx.reshape(bsz * CTX_LEN, d)

    cv = jnp.zeros((24, d), F32).at[:bsz].set(c).at[bsz].set(c_ctx)
    mod = _ada_mod(cv, w_ada, b_ada).reshape(depth, 24, N_MOD, d)
    rope_tabs = _rope_tables()
    dft_lat = _dft_tables(SEQ)
    dft_ctx = _dft_tables(CTX_LEN)

    for l in range(depth):
        last = l == depth - 1
        mod_l, mod_c = mod[l, :bsz], mod[l, bsz:bsz + 1]
        g = norm_g[l]
        w1, w3, w2 = ffn_w1[l].astype(BF16), ffn_w3[l].astype(BF16), ffn_w2[l].astype(BF16)
        wx = _extended_w_in(w_in[l])
        wo = w_out[l].astype(BF16)
        conv_p = jnp.stack([conv_b[l], conv_ln_g[l], conv_ln_b[l]])
        nat_bias = _nat_bias_table(nat_rel_bias[l])
        sink_lat = jnp.repeat(sink_logits[l], SWA_BLOCK)[:, None]
        sink_ctx = jnp.repeat(sink_logits[l], CTX_LEN)[:, None]

        xl1, ua, ub, qn, kn, vn, qs, ks, vs = _front(xl, mod_l, g, w1[0], w3[0], w2[0], wx, rope_tabs, SEQ, 256)
        xc1, ca, cb, cqn, ckn, cvn, cqs, cks, cvs = _front(xc, mod_c, g, w1[0], w3[0], w2[0],
                                                           wx[:, :N_EXT_CTX], None, bsz * CTX_LEN, 256)
        y_a = _conv(ua, conv_w[l], conv_p, SEQ)
        y_b = _fourier(ub, *dft_lat, SEQ)
        y_c, yc_c = _attention_call(_nat_kernel, "nat", (qn, kn, vn), (cqn, ckn, cvn), (nat_bias,), not last)
        y_d, yc_d = _attention_call(_swa_kernel, "swa", (qs, ks, vs), (cqs, cks, cvs), (sink_lat, sink_ctx),
                                    not last)
        xl = _back(xl1, (y_a, y_b, y_c, y_d), mod_l, g, wo, w1[1], w3[1], w2[1], SEQ, 256)
        if not last:
            yc_a = _conv(ca, conv_w[l], conv_p, CTX_LEN)
            yc_b = _fourier(cb, *dft_ctx, CTX_LEN)
            xc = _back(xc1, (yc_a, yc_b, yc_c, yc_d), mod_c, g, wo, w1[1], w3[1], w2[1], bsz * CTX_LEN, 256)
    return xl.reshape(bsz, seq, d)
```

```python
import functools

import numpy as np
import jax
import jax.numpy as jnp
from jax import lax
from jax.experimental import pallas as pl
from jax.experimental.pallas import tpu as pltpu

F32 = jnp.float32
BF16 = jnp.bfloat16

D_MODEL = 1024
SEQ = 2048
CTX_LEN = 256
GRID_W = 64
GRID_H = SEQ // GRID_W
GROUP_W = 256
HEAD_DIM = 64
N_HEADS = 4
CONV_K = 31
CONV_HALO = 16
NAT_ROWS = 8
NAT_COLS = 16
SWA_KV_HEADS = 2
SWA_WINDOW = 128
SWA_BLOCK = 128
ROPE_BASE = 10000.0
FFN_DIM = 2816
FFN_CHUNKS = 2
MACARON_W = 0.5
N_MOD = 9
EPS = 1e-6
NEG = -1e30
Q_SCALE = HEAD_DIM ** -0.5

COL_UA = 0
COL_UB = 512
COL_QN, COL_KN, COL_VN = 768, 1024, 1280
COL_QS, COL_KS, COL_VS = 1536, 1792, 2048
COL_QS_SW, COL_KS_SW = 2304, 2560
N_EXT_CTX = 2304
N_EXT = 2816

V7X_VMEM_LIMIT = 56 * 1024 * 1024
ROW_TILE = 512
SUB_TILES = 2


def _resident(shape):
    nd = len(shape)
    return pl.BlockSpec(shape, lambda *_: (0,) * nd, pipeline_mode=pl.Buffered(1))


def _params(n_axes=1, vmem=V7X_VMEM_LIMIT):
    return pltpu.CompilerParams(dimension_semantics=("arbitrary",) * n_axes, vmem_limit_bytes=vmem)


def _rms(x, g):
    return x * lax.rsqrt(jnp.mean(x * x, axis=-1, keepdims=True) + EPS) * g


def _sigmoid(x):
    return 1.0 / (1.0 + jnp.exp(-x))


def _ffn(x, shift, scale, gate, g_pre, g_post, w1_ref, w3_ref, w2_ref):
    h = (_rms(x, g_pre) * (1.0 + scale) + shift).astype(BF16)
    fc = FFN_DIM // FFN_CHUNKS
    y = None
    for c in range(FFN_CHUNKS):
        a = jnp.dot(h, w1_ref[:, c * fc:(c + 1) * fc], preferred_element_type=F32)
        b = jnp.dot(h, w3_ref[:, c * fc:(c + 1) * fc], preferred_element_type=F32)
        t = (a * _sigmoid(a) * b).astype(BF16)
        yc = jnp.dot(t, w2_ref[c * fc:(c + 1) * fc, :], preferred_element_type=F32)
        y = yc if y is None else y + yc
    return x + (MACARON_W * gate) * _rms(y, g_post)


def _ada_kernel(cv_ref, w_ref, b_ref, o_ref):
    c = cv_ref[...]
    sc = (c * _sigmoid(c)).astype(BF16)
    o_ref[...] = jnp.dot(sc, w_ref[...].astype(BF16), preferred_element_type=F32) + b_ref[...]


def _ada_mod(cv, w_ada, b_ada):
    depth, d, n = w_ada.shape
    rows = cv.shape[0]
    tn = 1152
    return pl.pallas_call(
        _ada_kernel,
        grid=(depth, n // tn),
        in_specs=[pl.BlockSpec((rows, d), lambda l, j: (0, 0)),
                  pl.BlockSpec((None, d, tn), lambda l, j: (l, 0, j)),
                  pl.BlockSpec((None, 1, tn), lambda l, j: (l, 0, j))],
        out_specs=pl.BlockSpec((None, rows, tn), lambda l, j: (l, 0, j)),
        out_shape=jax.ShapeDtypeStruct((depth, rows, n), F32),
        compiler_params=_params(2),
        name="ada_mod",
    )(cv, w_ada, b_ada.reshape(depth, 1, n))


def _front_kernel(*refs, rope, sub_tiles):
    if rope:
        (x_ref, mod_ref, g_ref, w1_ref, w3_ref, w2_ref, wx_ref, cos_ref, sin_ref,
         x1_ref, ua_ref, ub_ref, qn_ref, kn_ref, vn_ref, qs_ref, ks_ref, vs_ref) = refs
    else:
        (x_ref, mod_ref, g_ref, w1_ref, w3_ref, w2_ref, wx_ref,
         x1_ref, ua_ref, ub_ref, qn_ref, kn_ref, vn_ref, qs_ref, ks_ref, vs_ref) = refs
    ts = x_ref.shape[0] // sub_tiles
    for t in range(sub_tiles):
        rows = slice(t * ts, (t + 1) * ts)
        x1 = _ffn(x_ref[rows, :], mod_ref[0:1, :], mod_ref[1:2, :], mod_ref[2:3, :],
                  g_ref[0:1, :], g_ref[1:2, :], w1_ref, w3_ref, w2_ref)
        x1_ref[rows, :] = x1
        h = (_rms(x1, g_ref[2:3, :]) * (1.0 + mod_ref[4:5, :]) + mod_ref[3:4, :]).astype(BF16)
        u = jnp.dot(h, wx_ref[...], preferred_element_type=F32)
        ua_ref[rows, :] = u[:, COL_UA:COL_UA + 512]
        ub_ref[rows, :] = u[:, COL_UB:COL_UB + 256].astype(BF16)
        qn_ref[rows, :] = (u[:, COL_QN:COL_QN + 256] * Q_SCALE).astype(BF16)
        kn_ref[rows, :] = u[:, COL_KN:COL_KN + 256].astype(BF16)
        vn_ref[rows, :] = u[:, COL_VN:COL_VN + 256].astype(BF16)
        qs = u[:, COL_QS:COL_QS + 256]
        ks = u[:, COL_KS:COL_KS + 256]
        if rope:
            cos, sin = cos_ref[rows, :], sin_ref[rows, :]
            qs = qs * cos + u[:, COL_QS_SW:COL_QS_SW + 256] * sin
            ks = ks * cos + u[:, COL_KS_SW:COL_KS_SW + 256] * sin
        qs_ref[rows, :] = (qs * Q_SCALE).astype(BF16)
        ks_ref[rows, :] = ks.astype(BF16)
        vs_ref[rows, :] = u[:, COL_VS:COL_VS + 256].astype(BF16)


def _front(x, mod, g, w1, w3, w2, wx, rope_tabs, rows_per_batch, tm, sub_tiles):
    rows, d = x.shape
    tiles_per_batch = rows_per_batch // tm
    rope = rope_tabs is not None
    row_tile = lambda w: pl.BlockSpec((tm, w), lambda i: (i, 0))
    in_specs = [row_tile(d),
                pl.BlockSpec((None, N_MOD, d), lambda i: (i // tiles_per_batch, 0, 0)),
                _resident(g.shape), _resident(w1.shape), _resident(w3.shape), _resident(w2.shape),
                _resident(wx.shape)]
    args = [x, mod, g, w1, w3, w2, wx]
    if rope:
        tab = pl.BlockSpec((tm, 256), lambda i: (i % tiles_per_batch, 0))
        in_specs += [tab, tab]
        args += list(rope_tabs)
    out_shape = ([jax.ShapeDtypeStruct((rows, d), F32), jax.ShapeDtypeStruct((rows, 512), F32)]
                 + [jax.ShapeDtypeStruct((rows, 256), BF16)] * 7)
    out_specs = [row_tile(d), row_tile(512)] + [row_tile(256)] * 7
    return pl.pallas_call(
        functools.partial(_front_kernel, rope=rope, sub_tiles=sub_tiles),
        grid=(rows // tm,), in_specs=in_specs, out_specs=out_specs, out_shape=out_shape,
        compiler_params=_params(1), name="front_rope" if rope else "front_ctx",
    )(*args)


def _back_kernel(x_ref, ya_ref, yb_ref, yc_ref, yd_ref, mod_ref, g_ref, wo_ref, w1_ref, w3_ref, w2_ref, o_ref, *,
                 sub_tiles):
    ts = x_ref.shape[0] // sub_tiles
    for t in range(sub_tiles):
        rows = slice(t * ts, (t + 1) * ts)
        y = None
        for j, y_ref in enumerate((ya_ref, yb_ref, yc_ref, yd_ref)):
            yj = jnp.dot(y_ref[rows, :], wo_ref[j * GROUP_W:(j + 1) * GROUP_W, :], preferred_element_type=F32)
            y = yj if y is None else y + yj
        x2 = x_ref[rows, :] + mod_ref[5:6, :] * _rms(y, g_ref[3:4, :])
        o_ref[rows, :] = _ffn(x2, mod_ref[6:7, :], mod_ref[7:8, :], mod_ref[8:9, :],
                              g_ref[4:5, :], g_ref[5:6, :], w1_ref, w3_ref, w2_ref)


def _back(x, ys, mod, g, wo, w1, w3, w2, rows_per_batch, tm, sub_tiles):
    rows, d = x.shape
    tiles_per_batch = rows_per_batch // tm
    row_tile = lambda w: pl.BlockSpec((tm, w), lambda i: (i, 0))
    in_specs = ([row_tile(d)] + [row_tile(GROUP_W)] * 4
                + [pl.BlockSpec((None, N_MOD, d), lambda i: (i // tiles_per_batch, 0, 0)),
                   _resident(g.shape), _resident(wo.shape), _resident(w1.shape), _resident(w3.shape),
                   _resident(w2.shape)])
    return pl.pallas_call(
        functools.partial(_back_kernel, sub_tiles=sub_tiles), grid=(rows // tm,), in_specs=in_specs,
        out_specs=row_tile(d),
        out_shape=jax.ShapeDtypeStruct((rows, d), F32),
        compiler_params=_params(1), name="back",
    )(x, *ys, mod, g, wo, w1, w3, w2)


def _conv_kernel(prev_ref, cur_ref, next_ref, w_ref, p_ref, o_ref, pad_ref, *, n_chunks, rc, sub):
    c = pl.program_id(1)

    def glu(u):
        return u[:, :GROUP_W] * _sigmoid(u[:, GROUP_W:])

    pad_ref[0:CONV_HALO, :] = jnp.where(c > 0, glu(prev_ref[...]), 0.0)
    pad_ref[CONV_HALO:CONV_HALO + rc, :] = glu(cur_ref[...])
    pad_ref[CONV_HALO + rc:, :] = jnp.where(c < n_chunks - 1, glu(next_ref[...]), 0.0)
    bias, ln_g, ln_b = p_ref[0:1, :], p_ref[1:2, :], p_ref[2:3, :]
    lead = CONV_HALO - CONV_K // 2
    for s in range(rc // sub):
        acc = jnp.zeros((sub, GROUP_W), F32) + bias
        for k in range(CONV_K):
            r0 = s * sub + lead + k
            acc = acc + pad_ref[r0:r0 + sub, :] * w_ref[k:k + 1, :]
        mu = jnp.mean(acc, axis=-1, keepdims=True)
        cen = acc - mu
        var = jnp.mean(cen * cen, axis=-1, keepdims=True)
        yn = cen * lax.rsqrt(var + EPS) * ln_g + ln_b
        o_ref[s * sub:(s + 1) * sub, :] = (yn * _sigmoid(yn)).astype(BF16)


def _conv(ua, conv_w, conv_p, rows_per_batch):
    rows = ua.shape[0]
    batch = rows // rows_per_batch
    rc = 256
    n_chunks = rows_per_batch // rc
    hb = rc // CONV_HALO
    n_halo = rows // CONV_HALO
    cur = pl.BlockSpec((rc, 512), lambda b, c: (b * n_chunks + c, 0))
    prev = pl.BlockSpec((CONV_HALO, 512), lambda b, c: (jnp.maximum((b * n_chunks + c) * hb - 1, 0), 0))
    nxt = pl.BlockSpec((CONV_HALO, 512), lambda b, c: (jnp.minimum((b * n_chunks + c + 1) * hb, n_halo - 1), 0))
    return pl.pallas_call(
        functools.partial(_conv_kernel, n_chunks=n_chunks, rc=rc, sub=64),
        grid=(batch, n_chunks),
        in_specs=[prev, cur, nxt, _resident(conv_w.shape), _resident(conv_p.shape)],
        out_specs=pl.BlockSpec((rc, GROUP_W), lambda b, c: (b * n_chunks + c, 0)),
        out_shape=jax.ShapeDtypeStruct((rows, GROUP_W), BF16),
        scratch_shapes=[pltpu.VMEM((rc + 2 * CONV_HALO, GROUP_W), F32)],
        compiler_params=_params(2), name="conv",
    )(ua, ua, ua, conv_w, conv_p)


def _fourier_kernel(z_ref, wc_ref, m_ref, o_ref, *, n, scale):
    t = jnp.dot(z_ref[...], wc_ref[...], preferred_element_type=F32)
    zz = jnp.concatenate([t[:, :GROUP_W], t[:, GROUP_W:]], axis=0).astype(BF16)
    y = jnp.dot(m_ref[...], zz, preferred_element_type=F32)
    o_ref[...] = (y * scale).astype(BF16)


def _dft_tables(n):
    k = jnp.arange(n, dtype=jnp.int32)
    ang = (2.0 * np.pi / n) * ((k[:, None] * k[None, :]) % n).astype(F32)
    m = jnp.concatenate([jnp.cos(ang), -jnp.sin(ang)], axis=1).astype(BF16)
    c = jnp.arange(HEAD_DIM, dtype=jnp.int32)
    angc = (2.0 * np.pi / HEAD_DIM) * ((c[:, None] * c[None, :]) % HEAD_DIM).astype(F32)
    eye = jnp.eye(GROUP_W // HEAD_DIM, dtype=F32)
    wc = jnp.concatenate([jnp.kron(eye, jnp.cos(angc)), jnp.kron(eye, jnp.sin(angc))], axis=1).astype(BF16)
    return wc, m


def _fourier(ub, wc, m, rows_per_batch):
    rows = ub.shape[0]
    n = rows_per_batch
    blk = pl.BlockSpec((n, GROUP_W), lambda b: (b, 0))
    return pl.pallas_call(
        functools.partial(_fourier_kernel, n=n, scale=float((n * HEAD_DIM) ** -0.5)),
        grid=(rows // n,), in_specs=[blk, _resident(wc.shape), _resident(m.shape)], out_specs=blk,
        out_shape=jax.ShapeDtypeStruct((rows, GROUP_W), BF16),
        compiler_params=_params(1), name="fourier",
    )(ub, wc, m)


def _head_mask(rows_per_head):
    shape = (N_HEADS * rows_per_head, GROUP_W)
    row_h = lax.broadcasted_iota(jnp.int32, shape, 0) // rows_per_head
    lane_h = lax.broadcasted_iota(jnp.int32, shape, 1) // HEAD_DIM
    return (row_h == lane_h).astype(F32)


def _stack_heads(q, mask_bf16):
    return jnp.concatenate([q] * N_HEADS, axis=0) * mask_bf16


def _unstack_heads(o, mask_f32, rows_per_head):
    om = o * mask_f32
    out = om[0:rows_per_head]
    for h in range(1, N_HEADS):
        out = out + om[h * rows_per_head:(h + 1) * rows_per_head]
    return out


def _dot_nt(a, b):
    return lax.dot_general(a, b, (((1,), (1,)), ((), ())), preferred_element_type=F32)


def _ctx_dense(cq_ref, kx, vx, sink_col, oc_ref):
    mask = _head_mask(CTX_LEN)
    qst = _stack_heads(cq_ref[...], mask.astype(BF16))
    s = _dot_nt(qst, kx)
    m = jnp.max(s, axis=-1, keepdims=True)
    if sink_col is not None:
        m = jnp.maximum(m, sink_col)
    p = jnp.exp(s - m)
    den = jnp.sum(p, axis=-1, keepdims=True)
    if sink_col is not None:
        den = den + jnp.exp(sink_col - m)
    o = jnp.dot(p.astype(BF16), vx, preferred_element_type=F32) / den
    oc_ref[...] = _unstack_heads(o, mask, CTX_LEN).astype(BF16)


def _nat_kernel(*refs, ctx_out):
    if ctx_out:
        q_ref, k_ref, v_ref, kx_ref, vx_ref, bias_ref, cq_ref, o_ref, oc_ref = refs
    else:
        q_ref, k_ref, v_ref, kx_ref, vx_ref, bias_ref, o_ref = refs
    kx, vx = kx_ref[...], vx_ref[...]
    mask = _head_mask(GRID_W)
    mask_b = mask.astype(BF16)
    band = NAT_ROWS * GRID_W

    def row_step(r, carry):
        rs = jnp.clip(r - NAT_ROWS // 2, 0, GRID_H - NAT_ROWS)
        q0 = pl.multiple_of(r * GRID_W, GRID_W)
        k0 = pl.multiple_of(rs * GRID_W, GRID_W)
        qst = _stack_heads(q_ref[pl.ds(q0, GRID_W), :], mask_b)
        s = _dot_nt(qst, k_ref[pl.ds(k0, band), :]) + bias_ref[r - rs]
        sc = _dot_nt(qst, kx)
        m = jnp.maximum(jnp.max(s, axis=-1, keepdims=True), jnp.max(sc, axis=-1, keepdims=True))
        p = jnp.exp(s - m)
        pc = jnp.exp(sc - m)
        den = jnp.sum(p, axis=-1, keepdims=True) + jnp.sum(pc, axis=-1, keepdims=True)
        o = (jnp.dot(p.astype(BF16), v_ref[pl.ds(k0, band), :], preferred_element_type=F32)
             + jnp.dot(pc.astype(BF16), vx, preferred_element_type=F32)) / den
        o_ref[pl.ds(q0, GRID_W), :] = _unstack_heads(o, mask, GRID_W).astype(BF16)
        return carry

    lax.fori_loop(0, GRID_H, row_step, 0, unroll=4)
    if ctx_out:
        _ctx_dense(cq_ref, kx, vx, None, oc_ref)


def _nat_bias_table(rel_bias):
    w = GRID_W
    lo = w - NAT_COLS
    g = jnp.pad(rel_bias.astype(F32), ((0, 0), (0, 0), (lo, 2 * w - lo - (2 * NAT_COLS - 1))))
    g_off = jnp.stack([g[:, NAT_ROWS - 1 - off:2 * NAT_ROWS - 1 - off] for off in range(NAT_ROWS)])
    row = NAT_ROWS * 2 * w
    flat = jnp.tile(g_off.reshape(NAT_ROWS, N_HEADS, row), (1, 1, w))[:, :, :w * (row - 1)]
    skew = jnp.pad(flat.reshape(NAT_ROWS, N_HEADS, w, row - 1), ((0, 0), (0, 0), (0, 0), (0, 1)))
    b = skew.reshape(NAT_ROWS, N_HEADS, w, NAT_ROWS, 2 * w)[..., w - 1:2 * w - 1]
    qc = np.arange(w)[:, None]
    kc = np.arange(w)[None, :]
    ws = np.clip(qc - NAT_COLS // 2, 0, w - NAT_COLS)
    col_ok = (kc >= ws) & (kc < ws + NAT_COLS)
    b = jnp.where(col_ok[None, None, :, None, :], b, NEG)
    return b.reshape(NAT_ROWS, N_HEADS * w, NAT_ROWS * w)


def _attention_call(kernel, name, lat, ctx, consts, ctx_out, scratch=()):
    batch = lat[0].shape[0] // SEQ
    lat_blk = pl.BlockSpec((SEQ, GROUP_W), lambda b: (b, 0))
    ctx_blk = pl.BlockSpec((CTX_LEN, GROUP_W), lambda b: (b, 0))
    in_specs = [lat_blk] * 3 + [ctx_blk] * 2 + [_resident(c.shape) for c in consts]
    args = list(lat) + [ctx[1], ctx[2]] + list(consts)
    out_shape = [jax.ShapeDtypeStruct((batch * SEQ, GROUP_W), BF16)]
    out_specs = [lat_blk]
    if ctx_out:
        in_specs.append(ctx_blk)
        args.append(ctx[0])
        out_shape.append(jax.ShapeDtypeStruct((batch * CTX_LEN, GROUP_W), BF16))
        out_specs.append(ctx_blk)
    outs = pl.pallas_call(
        functools.partial(kernel, ctx_out=ctx_out), grid=(batch,), in_specs=in_specs,
        out_specs=out_specs, out_shape=out_shape, scratch_shapes=list(scratch),
        compiler_params=_params(1), name=name,
    )(*args)
    return outs if ctx_out else (outs[0], None)


def _swa_kernel(*refs, ctx_out):
    if ctx_out:
        q_ref, k_ref, v_ref, kx_ref, vx_ref, sink_ref, csink_ref, cq_ref, o_ref, oc_ref, win_ref = refs
    else:
        q_ref, k_ref, v_ref, kx_ref, vx_ref, sink_ref, csink_ref, o_ref, win_ref = refs
    kx, vx = kx_ref[...], vx_ref[...]
    sink = sink_ref[...]
    mask = _head_mask(SWA_BLOCK)
    mask_b = mask.astype(BF16)
    band = 3 * SWA_BLOCK
    n_blocks = SEQ // SWA_BLOCK
    shape = (N_HEADS * SWA_BLOCK, band)
    rel = (lax.broadcasted_iota(jnp.int32, shape, 1)
           - lax.broadcasted_iota(jnp.int32, shape, 0) % SWA_BLOCK)
    for i, shift in enumerate((0, -SWA_BLOCK, -2 * SWA_BLOCK)):
        win_ref[i] = jnp.where(jnp.abs(rel + shift) <= SWA_WINDOW, 0.0, NEG)

    def block_step(nb, carry):
        q0 = pl.multiple_of(nb * SWA_BLOCK, SWA_BLOCK)
        k0 = pl.multiple_of(jnp.clip(q0 - SWA_BLOCK, 0, SEQ - band), SWA_BLOCK)
        placement = jnp.where(nb == 0, 0, jnp.where(nb == n_blocks - 1, 2, 1))
        qst = _stack_heads(q_ref[pl.ds(q0, SWA_BLOCK), :], mask_b)
        s = _dot_nt(qst, k_ref[pl.ds(k0, band), :]) + win_ref[placement]
        sc = _dot_nt(qst, kx)
        m = jnp.maximum(jnp.maximum(jnp.max(s, axis=-1, keepdims=True), jnp.max(sc, axis=-1, keepdims=True)),
                        sink)
        p = jnp.exp(s - m)
        pc = jnp.exp(sc - m)
        den = jnp.sum(p, axis=-1, keepdims=True) + jnp.sum(pc, axis=-1, keepdims=True) + jnp.exp(sink - m)
        o = (jnp.dot(p.astype(BF16), v_ref[pl.ds(k0, band), :], preferred_element_type=F32)
             + jnp.dot(pc.astype(BF16), vx, preferred_element_type=F32)) / den
        o_ref[pl.ds(q0, SWA_BLOCK), :] = _unstack_heads(o, mask, SWA_BLOCK).astype(BF16)
        return carry

    lax.fori_loop(0, n_blocks, block_step, 0, unroll=2)
    if ctx_out:
        _ctx_dense(cq_ref, kx, vx, csink_ref[...], oc_ref)


def _rope_tables():
    t = np.arange(SEQ)
    pos = np.stack([t // GRID_W, t % GRID_W], axis=1).astype(np.float32)
    half = HEAD_DIM // 4
    inv = ROPE_BASE ** (-jnp.arange(half, dtype=F32) / half)
    ang = jnp.asarray(pos)[:, :, None] * inv[None, None, :]
    cos = jnp.cos(ang)
    sin = jnp.sin(ang)
    cos64 = jnp.concatenate([cos, cos], axis=-1).reshape(SEQ, HEAD_DIM)
    sin64 = jnp.concatenate([-sin, sin], axis=-1).reshape(SEQ, HEAD_DIM)
    return jnp.tile(cos64, (1, N_HEADS)), jnp.tile(sin64, (1, N_HEADS))


def _extended_w_in(w_in):
    d64 = np.arange(HEAD_DIM)
    swap64 = np.where(d64 % 32 < 16, d64 + 16, d64 - 16)
    kv_rep = np.concatenate([(h // (N_HEADS // SWA_KV_HEADS)) * HEAD_DIM + d64 for h in range(N_HEADS)])
    q_sw = np.concatenate([h * HEAD_DIM + swap64 for h in range(N_HEADS)])
    o_qs, o_ks, o_vs = 1536, 1792, 1920
    cols = np.concatenate([np.arange(0, 1536),
                           o_qs + np.arange(256), o_ks + kv_rep, o_vs + kv_rep,
                           o_qs + q_sw, o_ks + kv_rep[q_sw]])
    return w_in[:, cols].astype(BF16)


def kernel(x, c, ctx, c_ctx, w_ada, b_ada, norm_g, ffn_w1, ffn_w3, ffn_w2, w_in, conv_w, conv_b, conv_ln_g,
           conv_ln_b, nat_rel_bias, sink_logits, w_out):
    bsz, seq, d = x.shape
    depth = w_ada.shape[0]
    assert (seq, d, ctx.shape[1]) == (SEQ, D_MODEL, CTX_LEN)
    xl = x.reshape(bsz * seq, d)
    xc = ctx.reshape(bsz * CTX_LEN, d)

    cv = jnp.zeros((24, d), F32).at[:bsz].set(c).at[bsz].set(c_ctx)
    mod = _ada_mod(cv, w_ada, b_ada).reshape(depth, 24, N_MOD, d)
    rope_tabs = _rope_tables()
    dft_lat = _dft_tables(SEQ)
    dft_ctx = _dft_tables(CTX_LEN)

    for l in range(depth):
        last = l == depth - 1
        mod_l, mod_c = mod[l, :bsz], mod[l, bsz:bsz + 1]
        g = norm_g[l]
        w1, w3, w2 = ffn_w1[l].astype(BF16), ffn_w3[l].astype(BF16), ffn_w2[l].astype(BF16)
        wx = _extended_w_in(w_in[l])
        wo = w_out[l].astype(BF16)
        conv_p = jnp.stack([conv_b[l], conv_ln_g[l], conv_ln_b[l]])
        nat_bias = _nat_bias_table(nat_rel_bias[l])
        sink_lat = jnp.repeat(sink_logits[l], SWA_BLOCK)[:, None]
        sink_ctx = jnp.repeat(sink_logits[l], CTX_LEN)[:, None]

        xl1, ua, ub, qn, kn, vn, qs, ks, vs = _front(xl, mod_l, g, w1[0], w3[0], w2[0], wx, rope_tabs, SEQ,
                                                     ROW_TILE, SUB_TILES)
        xc1, ca, cb, cqn, ckn, cvn, cqs, cks, cvs = _front(xc, mod_c, g, w1[0], w3[0], w2[0], wx[:, :N_EXT_CTX],
                                                           None, bsz * CTX_LEN, ROW_TILE, SUB_TILES)
        y_a = _conv(ua, conv_w[l], conv_p, SEQ)
        y_b = _fourier(ub, *dft_lat, SEQ)
        y_c, yc_c = _attention_call(_nat_kernel, "nat", (qn, kn, vn), (cqn, ckn, cvn), (nat_bias,), not last)
        win = pltpu.VMEM((3, N_HEADS * SWA_BLOCK, 3 * SWA_BLOCK), F32)
        y_d, yc_d = _attention_call(_swa_kernel, "swa", (qs, ks, vs), (cqs, cks, cvs), (sink_lat, sink_ctx),
                                    not last, scratch=(win,))
        xl = _back(xl1, (y_a, y_b, y_c, y_d), mod_l, g, wo, w1[1], w3[1], w2[1], SEQ, ROW_TILE, SUB_TILES)
        if not last:
            yc_a = _conv(ca, conv_w[l], conv_p, CTX_LEN)
            yc_b = _fourier(cb, *dft_ctx, CTX_LEN)
            xc = _back(xc1, (yc_a, yc_b, yc_c, yc_d), mod_c, g, wo, w1[1], w3[1], w2[1], bsz * CTX_LEN,
                       ROW_TILE, SUB_TILES)
    return xl.reshape(bsz, seq, d)
```

```python
import functools

import numpy as np
import jax
import jax.numpy as jnp
from jax import lax
from jax.experimental import pallas as pl
from jax.experimental.pallas import tpu as pltpu

F32 = jnp.float32
BF16 = jnp.bfloat16

D_MODEL = 1024
SEQ = 2048
CTX_LEN = 256
GRID_W = 64
GRID_H = SEQ // GRID_W
GROUP_W = 256
HEAD_DIM = 64
N_HEADS = 4
CONV_K = 31
NAT_ROWS = 8
NAT_COLS = 16
SWA_KV_HEADS = 2
SWA_WINDOW = 128
SWA_BLOCK = 128
ROPE_BASE = 10000.0
FFN_DIM = 2816
MACARON_W = 0.5
N_MOD = 9
MOD_ROWS = 24
EPS = 1e-6
NEG = -1e30
Q_SCALE = HEAD_DIM ** -0.5

V7X_LANES = 128
V7X_SUBLANES = 8
V7X_MXU_DIM = 256
V7X_VMEM_LIMIT = 56 * 1024 * 1024
ROW_TILE = 512
SUB_TILES = 2
FFN_CHUNK_BOUNDS = (0, 6 * V7X_MXU_DIM, FFN_DIM)
CONV_HALO = 2 * V7X_SUBLANES

COL_UA = 0
COL_UB = 512
COL_QN, COL_KN, COL_VN = 768, 1024, 1280
COL_QS, COL_KS, COL_VS = 1536, 1792, 2048
COL_QS_SW, COL_KS_SW = 2304, 2560
N_EXT_CTX = 2304
N_EXT = 2816


def _resident(shape, lead=()):
    nd = len(shape)
    return pl.BlockSpec((None,) * len(lead) + tuple(shape), lambda *_: tuple(lead) + (0,) * nd,
                        pipeline_mode=pl.Buffered(1))


def _params(n_axes=1):
    return pltpu.CompilerParams(dimension_semantics=("arbitrary",) * n_axes, vmem_limit_bytes=V7X_VMEM_LIMIT)


def _rms(x, g):
    return x * lax.rsqrt(jnp.mean(x * x, axis=-1, keepdims=True) + EPS) * g


def _sigmoid(x):
    return 1.0 / (1.0 + jnp.exp(-x))


def _ffn(x, shift, scale, gate, g_pre, g_post, w1_ref, w3_ref, w2_ref):
    h = (_rms(x, g_pre) * (1.0 + scale) + shift).astype(BF16)
    y = None
    for lo, hi in zip(FFN_CHUNK_BOUNDS[:-1], FFN_CHUNK_BOUNDS[1:]):
        a = jnp.dot(h, w1_ref[:, lo:hi], preferred_element_type=F32)
        b = jnp.dot(h, w3_ref[:, lo:hi], preferred_element_type=F32)
        t = (a * _sigmoid(a) * b).astype(BF16)
        yc = jnp.dot(t, w2_ref[lo:hi, :], preferred_element_type=F32)
        y = yc if y is None else y + yc
    return x + (MACARON_W * gate) * _rms(y, g_post)


def _ada_kernel(cv_ref, w_ref, b_ref, o_ref):
    c = cv_ref[...]
    sc = (c * _sigmoid(c)).astype(BF16)
    o_ref[...] = jnp.dot(sc, w_ref[...].astype(BF16), preferred_element_type=F32) + b_ref[...]


def _ada_mod(cv, w_ada, b_ada):
    depth, d, n = w_ada.shape
    rows = cv.shape[0]
    tn = 9 * V7X_LANES
    return pl.pallas_call(
        _ada_kernel,
        grid=(depth, n // tn),
        in_specs=[pl.BlockSpec((rows, d), lambda l, j: (0, 0)),
                  pl.BlockSpec((None, d, tn), lambda l, j: (l, 0, j)),
                  pl.BlockSpec((None, 1, tn), lambda l, j: (l, 0, j))],
        out_specs=pl.BlockSpec((None, rows, tn), lambda l, j: (l, 0, j)),
        out_shape=jax.ShapeDtypeStruct((depth, rows, n), F32),
        compiler_params=_params(2),
        name="ada_mod",
    )(cv, w_ada, b_ada.reshape(depth, 1, n))


def _front_kernel(*refs, rope, sub_tiles):
    if rope:
        (x_ref, mod_ref, g_ref, w1_ref, w3_ref, w2_ref, wx_ref, cos_ref, sin_ref,
         x1_ref, ua_ref, ub_ref, qn_ref, kn_ref, vn_ref, qs_ref, ks_ref, vs_ref) = refs
    else:
        (x_ref, mod_ref, g_ref, w1_ref, w3_ref, w2_ref, wx_ref,
         x1_ref, ua_ref, ub_ref, qn_ref, kn_ref, vn_ref, qs_ref, ks_ref, vs_ref) = refs
    ts = x_ref.shape[0] // sub_tiles
    for t in range(sub_tiles):
        rows = slice(t * ts, (t + 1) * ts)
        x1 = _ffn(x_ref[rows, :], mod_ref[0:1, :], mod_ref[1:2, :], mod_ref[2:3, :],
                  g_ref[0:1, :], g_ref[1:2, :], w1_ref, w3_ref, w2_ref)
        x1_ref[rows, :] = x1
        h = (_rms(x1, g_ref[2:3, :]) * (1.0 + mod_ref[4:5, :]) + mod_ref[3:4, :]).astype(BF16)
        u = jnp.dot(h, wx_ref[...], preferred_element_type=F32)
        ua_ref[rows, :] = u[:, COL_UA:COL_UA + 512]
        ub_ref[rows, :] = u[:, COL_UB:COL_UB + 256].astype(BF16)
        qn_ref[rows, :] = (u[:, COL_QN:COL_QN + 256] * Q_SCALE).astype(BF16)
        kn_ref[rows, :] = u[:, COL_KN:COL_KN + 256].astype(BF16)
        vn_ref[rows, :] = u[:, COL_VN:COL_VN + 256].astype(BF16)
        qs = u[:, COL_QS:COL_QS + 256]
        ks = u[:, COL_KS:COL_KS + 256]
        if rope:
            cos, sin = cos_ref[rows, :], sin_ref[rows, :]
            qs = qs * cos + u[:, COL_QS_SW:COL_QS_SW + 256] * sin
            ks = ks * cos + u[:, COL_KS_SW:COL_KS_SW + 256] * sin
        qs_ref[rows, :] = (qs * Q_SCALE).astype(BF16)
        ks_ref[rows, :] = ks.astype(BF16)
        vs_ref[rows, :] = u[:, COL_VS:COL_VS + 256].astype(BF16)


def _mod_spec(layer, row0, tiles_per_batch):
    return pl.BlockSpec((None, None, N_MOD, D_MODEL), lambda i: (layer, row0 + i // tiles_per_batch, 0, 0))


def _front(x, mod, mod_row0, rows_per_batch, layer, g, w1, w3, w2, wx, rope_tabs):
    rows, d = x.shape
    tm = ROW_TILE
    tiles_per_batch = rows_per_batch // tm
    rope = rope_tabs is not None
    n_ext = N_EXT if rope else N_EXT_CTX
    row_tile = lambda w: pl.BlockSpec((tm, w), lambda i: (i, 0))
    in_specs = [row_tile(d), _mod_spec(layer, mod_row0, tiles_per_batch),
                _resident(g.shape[1:], (layer,)), _resident(w1.shape[2:], (layer, 0)),
                _resident(w3.shape[2:], (layer, 0)), _resident(w2.shape[2:], (layer, 0)),
                _resident((d, n_ext), (layer,))]
    args = [x, mod, g, w1, w3, w2, wx]
    if rope:
        tab = pl.BlockSpec((tm, 256), lambda i: (i % tiles_per_batch, 0))
        in_specs += [tab, tab]
        args += list(rope_tabs)
    out_shape = ([jax.ShapeDtypeStruct((rows, d), F32), jax.ShapeDtypeStruct((rows, 512), F32)]
                 + [jax.ShapeDtypeStruct((rows, 256), BF16)] * 7)
    out_specs = [row_tile(d), row_tile(512)] + [row_tile(256)] * 7
    return pl.pallas_call(
        functools.partial(_front_kernel, rope=rope, sub_tiles=SUB_TILES),
        grid=(rows // tm,), in_specs=in_specs, out_specs=out_specs, out_shape=out_shape,
        compiler_params=_params(1), name="front_rope" if rope else "front_ctx",
    )(*args)


def _back_kernel(x_ref, ya_ref, yb_ref, yc_ref, yd_ref, mod_ref, g_ref, wo_ref, w1_ref, w3_ref, w2_ref, o_ref, *,
                 sub_tiles):
    ts = x_ref.shape[0] // sub_tiles
    for t in range(sub_tiles):
        rows = slice(t * ts, (t + 1) * ts)
        y = None
        for j, y_ref in enumerate((ya_ref, yb_ref, yc_ref, yd_ref)):
            yj = jnp.dot(y_ref[rows, :], wo_ref[j * GROUP_W:(j + 1) * GROUP_W, :], preferred_element_type=F32)
            y = yj if y is None else y + yj
        x2 = x_ref[rows, :] + mod_ref[5:6, :] * _rms(y, g_ref[3:4, :])
        o_ref[rows, :] = _ffn(x2, mod_ref[6:7, :], mod_ref[7:8, :], mod_ref[8:9, :],
                              g_ref[4:5, :], g_ref[5:6, :], w1_ref, w3_ref, w2_ref)


def _back(x, ys, mod, mod_row0, rows_per_batch, layer, g, wo, w1, w3, w2):
    rows, d = x.shape
    tm = ROW_TILE
    tiles_per_batch = rows_per_batch // tm
    row_tile = lambda w: pl.BlockSpec((tm, w), lambda i: (i, 0))
    in_specs = ([row_tile(d)] + [row_tile(GROUP_W)] * 4
                + [_mod_spec(layer, mod_row0, tiles_per_batch),
                   _resident(g.shape[1:], (layer,)), _resident(wo.shape[1:], (layer,)),
                   _resident(w1.shape[2:], (layer, 1)), _resident(w3.shape[2:], (layer, 1)),
                   _resident(w2.shape[2:], (layer, 1))])
    return pl.pallas_call(
        functools.partial(_back_kernel, sub_tiles=SUB_TILES), grid=(rows // tm,), in_specs=in_specs,
        out_specs=row_tile(d), out_shape=jax.ShapeDtypeStruct((rows, d), F32),
        compiler_params=_params(1), name="back",
    )(x, *ys, mod, g, wo, w1, w3, w2)


def _conv_kernel(prev_ref, cur_ref, next_ref, w_ref, p_ref, o_ref, pad_ref, *, n_chunks, rc, sub):
    c = pl.program_id(1)

    def glu(u):
        return u[:, :GROUP_W] * _sigmoid(u[:, GROUP_W:])

    pad_ref[0:CONV_HALO, :] = jnp.where(c > 0, glu(prev_ref[...]), 0.0)
    pad_ref[CONV_HALO:CONV_HALO + rc, :] = glu(cur_ref[...])
    pad_ref[CONV_HALO + rc:, :] = jnp.where(c < n_chunks - 1, glu(next_ref[...]), 0.0)
    bias, ln_g, ln_b = p_ref[0:1, :], p_ref[1:2, :], p_ref[2:3, :]
    lead = CONV_HALO - CONV_K // 2
    zrows = sub + V7X_SUBLANES
    for s in range(rc // sub):
        acc = jnp.zeros((sub, GROUP_W), F32) + bias
        for b in range(V7X_SUBLANES):
            z = None
            for a in range((lead + CONV_K - 1) // V7X_SUBLANES + 1):
                k = V7X_SUBLANES * a + b - lead
                if 0 <= k < CONV_K:
                    r0 = s * sub + V7X_SUBLANES * a
                    term = pad_ref[r0:r0 + zrows, :] * w_ref[k:k + 1, :]
                    z = term if z is None else z + term
            acc = acc + z[b:b + sub, :]
        mu = jnp.mean(acc, axis=-1, keepdims=True)
        cen = acc - mu
        var = jnp.mean(cen * cen, axis=-1, keepdims=True)
        yn = cen * lax.rsqrt(var + EPS) * ln_g + ln_b
        o_ref[s * sub:(s + 1) * sub, :] = (yn * _sigmoid(yn)).astype(BF16)


def _conv(ua, conv_w, conv_p, layer, rows_per_batch):
    rows = ua.shape[0]
    batch = rows // rows_per_batch
    rc = 256
    n_chunks = rows_per_batch // rc
    hb = rc // CONV_HALO
    n_halo = rows // CONV_HALO
    cur = pl.BlockSpec((rc, 512), lambda b, c: (b * n_chunks + c, 0))
    prev = pl.BlockSpec((CONV_HALO, 512), lambda b, c: (jnp.maximum((b * n_chunks + c) * hb - 1, 0), 0))
    nxt = pl.BlockSpec((CONV_HALO, 512), lambda b, c: (jnp.minimum((b * n_chunks + c + 1) * hb, n_halo - 1), 0))
    return pl.pallas_call(
        functools.partial(_conv_kernel, n_chunks=n_chunks, rc=rc, sub=64),
        grid=(batch, n_chunks),
        in_specs=[prev, cur, nxt, _resident(conv_w.shape[1:], (layer,)), _resident(conv_p.shape[1:], (layer,))],
        out_specs=pl.BlockSpec((rc, GROUP_W), lambda b, c: (b * n_chunks + c, 0)),
        out_shape=jax.ShapeDtypeStruct((rows, GROUP_W), BF16),
        scratch_shapes=[pltpu.VMEM((rc + 2 * CONV_HALO, GROUP_W), F32)],
        compiler_params=_params(2), name="conv",
    )(ua, ua, ua, conv_w, conv_p)


def _fourier_kernel(z_ref, wc_ref, m_ref, o_ref, *, scale):
    t = jnp.dot(z_ref[...], wc_ref[...], preferred_element_type=F32)
    zz = jnp.concatenate([t[:, :GROUP_W], t[:, GROUP_W:]], axis=0).astype(BF16)
    y = jnp.dot(m_ref[...], zz, preferred_element_type=F32)
    o_ref[...] = (y * scale).astype(BF16)


def _unit_circle(n, period):
    i = jnp.arange(n, dtype=jnp.int32)[:, None]
    j = jnp.arange(period, dtype=jnp.int32)[None, :]
    ang = (2.0 * np.pi / period) * ((i * j) % period).astype(F32)
    return jnp.cos(ang), jnp.sin(ang)


def _dft_tables(n):
    hi = max(n // GRID_W, 1)
    lo = n // hi
    k = jnp.arange(n, dtype=jnp.int32)[None, :]
    n1 = jnp.arange(hi, dtype=jnp.int32)[:, None]
    n0 = jnp.arange(lo, dtype=jnp.int32)[:, None]
    a1 = (2.0 * np.pi / hi) * ((n1 * k) % hi).astype(F32)
    a0 = (2.0 * np.pi / n) * ((n0 * k) % n).astype(F32)
    c1, s1, c0, s0 = jnp.cos(a1)[:, None], jnp.sin(a1)[:, None], jnp.cos(a0)[None], jnp.sin(a0)[None]
    cos = (c1 * c0 - s1 * s0).reshape(n, n)
    sin = (s1 * c0 + c1 * s0).reshape(n, n)
    m = jnp.concatenate([cos, -sin], axis=1).astype(BF16)
    cc, sc = _unit_circle(HEAD_DIM, HEAD_DIM)
    eye = jnp.eye(GROUP_W // HEAD_DIM, dtype=F32)
    wc = jnp.concatenate([jnp.kron(eye, cc), jnp.kron(eye, sc)], axis=1).astype(BF16)
    return wc, m


def _fourier(ub, wc, m, rows_per_batch):
    rows = ub.shape[0]
    n = rows_per_batch
    blk = pl.BlockSpec((n, GROUP_W), lambda b: (b, 0))
    return pl.pallas_call(
        functools.partial(_fourier_kernel, scale=float((n * HEAD_DIM) ** -0.5)),
        grid=(rows // n,), in_specs=[blk, _resident(wc.shape), _resident(m.shape)], out_specs=blk,
        out_shape=jax.ShapeDtypeStruct((rows, GROUP_W), BF16),
        compiler_params=_params(1), name="fourier",
    )(ub, wc, m)


def _head_mask(rows_per_head, heads=N_HEADS, head0=0):
    shape = (heads * rows_per_head, GROUP_W)
    row_h = lax.broadcasted_iota(jnp.int32, shape, 0) // rows_per_head + head0
    lane_h = lax.broadcasted_iota(jnp.int32, shape, 1) // HEAD_DIM
    return (row_h == lane_h).astype(F32)


def _stack_heads(q, mask_bf16):
    heads = mask_bf16.shape[0] // q.shape[0]
    return jnp.concatenate([q] * heads, axis=0) * mask_bf16


def _unstack_heads(o, mask_f32, rows_per_head):
    om = o * mask_f32
    out = om[0:rows_per_head]
    for h in range(1, o.shape[0] // rows_per_head):
        out = out + om[h * rows_per_head:(h + 1) * rows_per_head]
    return out


def _dot_nt(a, b):
    return lax.dot_general(a, b, (((1,), (1,)), ((), ())), preferred_element_type=F32)


def _tree(op, xs):
    while len(xs) > 1:
        xs = [op(xs[i], xs[i + 1]) if i + 1 < len(xs) else xs[i] for i in range(0, len(xs), 2)]
    return xs[0]


def _lane_tiles(*arrays):
    return [a[:, i:i + V7X_LANES] for a in arrays for i in range(0, a.shape[1], V7X_LANES)]


def _softmax_pv(s, sc, vb, vx, sink):
    m = jnp.max(_tree(jnp.maximum, _lane_tiles(s, sc)), axis=-1, keepdims=True)
    if sink is not None:
        m = jnp.maximum(m, sink)
    p = jnp.exp(s - m)
    pc = jnp.exp(sc - m)
    den = jnp.sum(_tree(jnp.add, _lane_tiles(p, pc)), axis=-1, keepdims=True)
    if sink is not None:
        den = den + jnp.exp(sink - m)
    o = (jnp.dot(p.astype(BF16), vb, preferred_element_type=F32)
         + jnp.dot(pc.astype(BF16), vx, preferred_element_type=F32))
    return o / den


def _ctx_dense(cq_ref, kx, vx, sink_ref, oc_ref):
    hpc = 2
    out = None
    for g in range(N_HEADS // hpc):
        mask = _head_mask(CTX_LEN, hpc, g * hpc)
        qst = _stack_heads(cq_ref[...], mask.astype(BF16))
        s = _dot_nt(qst, kx)
        m = jnp.max(_tree(jnp.maximum, _lane_tiles(s)), axis=-1, keepdims=True)
        sink = None if sink_ref is None else sink_ref[g * hpc * CTX_LEN:(g + 1) * hpc * CTX_LEN, :]
        if sink is not None:
            m = jnp.maximum(m, sink)
        p = jnp.exp(s - m)
        den = jnp.sum(_tree(jnp.add, _lane_tiles(p)), axis=-1, keepdims=True)
        if sink is not None:
            den = den + jnp.exp(sink - m)
        o = jnp.dot(p.astype(BF16), vx, preferred_element_type=F32) / den
        og = _unstack_heads(o, mask, CTX_LEN)
        out = og if out is None else out + og
    oc_ref[...] = out.astype(BF16)


def _nat_kernel(*refs, ctx_out):
    if ctx_out:
        q_ref, k_ref, v_ref, kx_ref, vx_ref, bias_ref, cq_ref, o_ref, oc_ref = refs
    else:
        q_ref, k_ref, v_ref, kx_ref, vx_ref, bias_ref, o_ref = refs
    kx, vx = kx_ref[...], vx_ref[...]
    mask = _head_mask(GRID_W)
    mask_b = mask.astype(BF16)
    band = NAT_ROWS * GRID_W

    def row_step(r, carry):
        rs = jnp.clip(r - NAT_ROWS // 2, 0, GRID_H - NAT_ROWS)
        q0 = pl.multiple_of(r * GRID_W, GRID_W)
        k0 = pl.multiple_of(rs * GRID_W, GRID_W)
        d0 = rs - r + NAT_ROWS - 1
        qst = _stack_heads(q_ref[pl.ds(q0, GRID_W), :], mask_b)
        bias = jnp.concatenate([bias_ref[d0 + 2 * a] for a in range(NAT_ROWS // 2)], axis=1)
        s = _dot_nt(qst, k_ref[pl.ds(k0, band), :]) + bias
        sc = _dot_nt(qst, kx)
        o = _softmax_pv(s, sc, v_ref[pl.ds(k0, band), :], vx, None)
        o_ref[pl.ds(q0, GRID_W), :] = _unstack_heads(o, mask, GRID_W).astype(BF16)
        return carry

    lax.fori_loop(0, GRID_H, row_step, 0, unroll=4)
    if ctx_out:
        _ctx_dense(cq_ref, kx, vx, None, oc_ref)


def _nat_bias_table(rel_bias):
    depth = rel_bias.shape[0]
    w = GRID_W
    n_dr = 2 * NAT_ROWS - 1
    lo = w - NAT_COLS
    g = jnp.pad(rel_bias.astype(F32), ((0, 0),) * 3 + ((lo, 2 * w - lo - (2 * NAT_COLS - 1)),))
    flat = jnp.tile(g, (1, 1, 1, w))[..., :w * (2 * w - 1)]
    t = flat.reshape(depth, N_HEADS, n_dr, w, 2 * w - 1)[..., w - 1:]
    qc = np.arange(w)[:, None]
    kc = np.arange(w)[None, :]
    ws = np.clip(qc - NAT_COLS // 2, 0, w - NAT_COLS)
    col_ok = (kc >= ws) & (kc < ws + NAT_COLS)
    t = jnp.where(col_ok, t, NEG).transpose(0, 2, 1, 3, 4).reshape(depth, n_dr, N_HEADS * w, w)
    return jnp.concatenate([t[:, :-1], t[:, 1:]], axis=-1)


def _attention_call(kernel, name, lat, ctx, consts, ctx_out, scratch=()):
    batch = lat[0].shape[0] // SEQ
    lat_blk = pl.BlockSpec((SEQ, GROUP_W), lambda b: (b, 0))
    ctx_blk = pl.BlockSpec((CTX_LEN, GROUP_W), lambda b: (b, 0))
    in_specs = [lat_blk] * 3 + [ctx_blk] * 2 + [_resident(c.shape[len(lead):], lead) for c, lead in consts]
    args = list(lat) + [ctx[1], ctx[2]] + [c for c, _ in consts]
    out_shape = [jax.ShapeDtypeStruct((batch * SEQ, GROUP_W), BF16)]
    out_specs = [lat_blk]
    if ctx_out:
        in_specs.append(ctx_blk)
        args.append(ctx[0])
        out_shape.append(jax.ShapeDtypeStruct((batch * CTX_LEN, GROUP_W), BF16))
        out_specs.append(ctx_blk)
    outs = pl.pallas_call(
        functools.partial(kernel, ctx_out=ctx_out), grid=(batch,), in_specs=in_specs,
        out_specs=out_specs, out_shape=out_shape, scratch_shapes=list(scratch),
        compiler_params=_params(1), name=name,
    )(*args)
    return outs if ctx_out else (outs[0], None)


SWA_HEADS_PER_CHAIN = 2


def _swa_kernel(*refs, ctx_out):
    if ctx_out:
        q_ref, k_ref, v_ref, kx_ref, vx_ref, sink_ref, csink_ref, cq_ref, o_ref, oc_ref, win_ref = refs
    else:
        q_ref, k_ref, v_ref, kx_ref, vx_ref, sink_ref, csink_ref, o_ref, win_ref = refs
    kx, vx = kx_ref[...], vx_ref[...]
    hpc = SWA_HEADS_PER_CHAIN
    chain_rows = hpc * SWA_BLOCK
    band = 3 * SWA_BLOCK
    n_blocks = SEQ // SWA_BLOCK

    @pl.when(pl.program_id(0) == 0)
    def _():
        shape = (chain_rows, band)
        rel = (lax.broadcasted_iota(jnp.int32, shape, 1)
               - lax.broadcasted_iota(jnp.int32, shape, 0) % SWA_BLOCK)
        for i, shift in enumerate((0, -SWA_BLOCK, -2 * SWA_BLOCK)):
            win_ref[i] = jnp.where(jnp.abs(rel + shift) <= SWA_WINDOW, 0.0, NEG)

    masks = [_head_mask(SWA_BLOCK, hpc, g * hpc) for g in range(N_HEADS // hpc)]
    masks_b = [mask.astype(BF16) for mask in masks]

    def block_step(nb, carry):
        q0 = pl.multiple_of(nb * SWA_BLOCK, SWA_BLOCK)
        k0 = pl.multiple_of(jnp.clip(q0 - SWA_BLOCK, 0, SEQ - band), SWA_BLOCK)
        placement = jnp.where(nb == 0, 0, jnp.where(nb == n_blocks - 1, 2, 1))
        q = q_ref[pl.ds(q0, SWA_BLOCK), :]
        out = None
        for g, mask in enumerate(masks):
            qst = _stack_heads(q, masks_b[g])
            s = _dot_nt(qst, k_ref[pl.ds(k0, band), :]) + win_ref[placement]
            sc = _dot_nt(qst, kx)
            o = _softmax_pv(s, sc, v_ref[pl.ds(k0, band), :], vx,
                            sink_ref[g * chain_rows:(g + 1) * chain_rows, :])
            og = _unstack_heads(o, mask, SWA_BLOCK)
            out = og if out is None else out + og
        o_ref[pl.ds(q0, SWA_BLOCK), :] = out.astype(BF16)
        return carry

    lax.fori_loop(0, n_blocks, block_step, 0, unroll=2)
    if ctx_out:
        _ctx_dense(cq_ref, kx, vx, csink_ref, oc_ref)


def _rope_tables():
    t = np.arange(SEQ)
    pos = np.stack([t // GRID_W, t % GRID_W], axis=1).astype(np.float32)
    half = HEAD_DIM // 4
    inv = ROPE_BASE ** (-jnp.arange(half, dtype=F32) / half)
    ang = jnp.asarray(pos)[:, :, None] * inv[None, None, :]
    cos = jnp.cos(ang)
    sin = jnp.sin(ang)
    cos64 = jnp.concatenate([cos, cos], axis=-1).reshape(SEQ, HEAD_DIM)
    sin64 = jnp.concatenate([-sin, sin], axis=-1).reshape(SEQ, HEAD_DIM)
    return jnp.tile(cos64, (1, N_HEADS)), jnp.tile(sin64, (1, N_HEADS))


def _extended_w_in(w_in):
    lead = w_in.shape[:-1]
    group = N_HEADS // SWA_KV_HEADS

    def per_query_head(w):
        return jnp.repeat(w.reshape(lead + (SWA_KV_HEADS, HEAD_DIM)), group, axis=-2).reshape(lead + (GROUP_W,))

    def rotate_half_partner(w):
        return w.reshape(lead + (N_HEADS, 2, 2, HEAD_DIM // 4))[..., ::-1, :].reshape(lead + (GROUP_W,))

    qs = w_in[..., 1536:1792]
    ks = per_query_head(w_in[..., 1792:1920])
    vs = per_query_head(w_in[..., 1920:2048])
    return jnp.concatenate([w_in[..., :1792], ks, vs, rotate_half_partner(qs), rotate_half_partner(ks)],
                           axis=-1).astype(BF16)


def kernel(x, c, ctx, c_ctx, w_ada, b_ada, norm_g, ffn_w1, ffn_w3, ffn_w2, w_in, conv_w, conv_b, conv_ln_g,
           conv_ln_b, nat_rel_bias, sink_logits, w_out):
    bsz, seq, d = x.shape
    depth = w_ada.shape[0]
    assert (seq, d, ctx.shape[1]) == (SEQ, D_MODEL, CTX_LEN) and bsz < MOD_ROWS
    xl = x.reshape(bsz * seq, d)
    xc = ctx.reshape(bsz * CTX_LEN, d)

    cv = jnp.zeros((MOD_ROWS, d), F32).at[:bsz].set(c).at[bsz].set(c_ctx)
    mod = _ada_mod(cv, w_ada, b_ada).reshape(depth, MOD_ROWS, N_MOD, d)
    rope_tabs = _rope_tables()
    dft_lat = _dft_tables(SEQ)
    dft_ctx = _dft_tables(CTX_LEN)
    w1, w3, w2 = ffn_w1.astype(BF16), ffn_w3.astype(BF16), ffn_w2.astype(BF16)
    wx = _extended_w_in(w_in)
    wo = w_out.astype(BF16)
    conv_p = jnp.stack([conv_b, conv_ln_g, conv_ln_b], axis=1)
    nat_bias = _nat_bias_table(nat_rel_bias)
    sink_lat = jnp.repeat(sink_logits, SWA_BLOCK, axis=1)[..., None]
    sink_ctx = jnp.repeat(sink_logits, CTX_LEN, axis=1)[..., None]
    win = pltpu.VMEM((3, SWA_HEADS_PER_CHAIN * SWA_BLOCK, 3 * SWA_BLOCK), F32)

    for l in range(depth):
        last = l == depth - 1
        xl1, ua, ub, qn, kn, vn, qs, ks, vs = _front(xl, mod, 0, SEQ, l, norm_g, w1, w3, w2, wx, rope_tabs)
        xc1, ca, cb, cqn, ckn, cvn, cqs, cks, cvs = _front(xc, mod, bsz, bsz * CTX_LEN, l, norm_g, w1, w3, w2,
                                                           wx, None)
        y_a = _conv(ua, conv_w, conv_p, l, SEQ)
        y_b = _fourier(ub, *dft_lat, SEQ)
        y_c, yc_c = _attention_call(_nat_kernel, "nat", (qn, kn, vn), (cqn, ckn, cvn), ((nat_bias, (l,)),),
                                    not last)
        y_d, yc_d = _attention_call(_swa_kernel, "swa", (qs, ks, vs), (cqs, cks, cvs),
                                    ((sink_lat, (l,)), (sink_ctx, (l,))), not last, scratch=(win,))
        xl = _back(xl1, (y_a, y_b, y_c, y_d), mod, 0, SEQ, l, norm_g, wo, w1, w3, w2)
        if not last:
            yc_a = _conv(ca, conv_w, conv_p, l, CTX_LEN)
            yc_b = _fourier(cb, *dft_ctx, CTX_LEN)
            xc = _back(xc1, (yc_a, yc_b, yc_c, yc_d), mod, bsz, bsz * CTX_LEN, l, norm_g, wo, w1, w3, w2)
    return xl.reshape(bsz, seq, d)
```

```python
import functools

import numpy as np
import jax
import jax.numpy as jnp
from jax import lax
from jax.experimental import pallas as pl
from jax.experimental.pallas import tpu as pltpu

F32 = jnp.float32
BF16 = jnp.bfloat16

D_MODEL = 1024
SEQ = 2048
CTX_LEN = 256
GRID_W = 64
GRID_H = SEQ // GRID_W
GROUP_W = 256
HEAD_DIM = 64
N_HEADS = 4
CONV_K = 31
NAT_ROWS = 8
NAT_COLS = 16
SWA_KV_HEADS = 2
SWA_WINDOW = 128
SWA_BLOCK = 128
ROPE_BASE = 10000.0
FFN_DIM = 2816
MACARON_W = 0.5
N_MOD = 9
MOD_ROWS = 24
EPS = 1e-6
NEG = -1e30
LOG2_E = float(np.log2(np.e))
Q_SCALE = HEAD_DIM ** -0.5 * LOG2_E

V7X_LANES = 128
V7X_SUBLANES = 8
V7X_MXU_DIM = 256
V7X_VMEM_LIMIT = 56 * 1024 * 1024
ROW_TILE = 512
SUB_TILES = 2
FFN_CHUNK_BOUNDS = (0, 6 * V7X_MXU_DIM, FFN_DIM)
CONV_HALO = 2 * V7X_SUBLANES

COL_UA = 0
COL_UB = 512
COL_QN, COL_KN, COL_VN = 768, 1024, 1280
COL_QS, COL_KS, COL_VS = 1536, 1792, 2048
COL_QS_SW, COL_KS_SW = 2304, 2560
N_EXT_CTX = 2304
N_EXT = 2816


def _resident(shape, lead=()):
    nd = len(shape)
    return pl.BlockSpec((None,) * len(lead) + tuple(shape), lambda *_: tuple(lead) + (0,) * nd,
                        pipeline_mode=pl.Buffered(1))


def _params(n_axes=1):
    return pltpu.CompilerParams(dimension_semantics=("arbitrary",) * n_axes, vmem_limit_bytes=V7X_VMEM_LIMIT)


def _rms(x, g):
    return x * lax.rsqrt(jnp.mean(x * x, axis=-1, keepdims=True) + EPS) * g


def _sigmoid(x):
    return 1.0 / (1.0 + jnp.exp(-x))


def _ffn(xs, shift, scale, gate, g_pre, g_post, w1_ref, w3_ref, w2_ref):
    hs = [(_rms(x, g_pre) * (1.0 + scale) + shift).astype(BF16) for x in xs]
    ys = [None] * len(xs)
    for lo, hi in zip(FFN_CHUNK_BOUNDS[:-1], FFN_CHUNK_BOUNDS[1:]):
        ts = []
        for h in hs:
            a = jnp.dot(h, w1_ref[:, lo:hi], preferred_element_type=F32)
            b = jnp.dot(h, w3_ref[:, lo:hi], preferred_element_type=F32)
            ts.append((a * _sigmoid(a) * b).astype(BF16))
        for i, t in enumerate(ts):
            yc = jnp.dot(t, w2_ref[lo:hi, :], preferred_element_type=F32)
            ys[i] = yc if ys[i] is None else ys[i] + yc
    return [x + (MACARON_W * gate) * _rms(y, g_post) for x, y in zip(xs, ys)]


def _ada_kernel(cv_ref, w_ref, b_ref, o_ref):
    c = cv_ref[...]
    sc = (c * _sigmoid(c)).astype(BF16)
    o_ref[...] = jnp.dot(sc, w_ref[...].astype(BF16), preferred_element_type=F32) + b_ref[...]


def _ada_mod(cv, w_ada, b_ada):
    depth, d, n = w_ada.shape
    rows = cv.shape[0]
    tn = 9 * V7X_LANES
    return pl.pallas_call(
        _ada_kernel,
        grid=(depth, n // tn),
        in_specs=[pl.BlockSpec((rows, d), lambda l, j: (0, 0)),
                  pl.BlockSpec((None, d, tn), lambda l, j: (l, 0, j)),
                  pl.BlockSpec((None, 1, tn), lambda l, j: (l, 0, j))],
        out_specs=pl.BlockSpec((None, rows, tn), lambda l, j: (l, 0, j)),
        out_shape=jax.ShapeDtypeStruct((depth, rows, n), F32),
        compiler_params=_params(2),
        name="ada_mod",
    )(cv, w_ada, b_ada.reshape(depth, 1, n))


def _front_kernel(*refs, rope, sub_tiles):
    if rope:
        (x_ref, mod_ref, g_ref, w1_ref, w3_ref, w2_ref, wx_ref, cos_ref, sin_ref,
         x1_ref, ua_ref, ub_ref, qn_ref, kn_ref, vn_ref, qs_ref, ks_ref, vs_ref) = refs
    else:
        (x_ref, mod_ref, g_ref, w1_ref, w3_ref, w2_ref, wx_ref,
         x1_ref, ua_ref, ub_ref, qn_ref, kn_ref, vn_ref, qs_ref, ks_ref, vs_ref) = refs
    ts = x_ref.shape[0] // sub_tiles
    tiles = [slice(t * ts, (t + 1) * ts) for t in range(sub_tiles)]
    x1s = _ffn([x_ref[rows, :] for rows in tiles], mod_ref[0:1, :], mod_ref[1:2, :], mod_ref[2:3, :],
               g_ref[0:1, :], g_ref[1:2, :], w1_ref, w3_ref, w2_ref)
    hs = []
    for rows, x1 in zip(tiles, x1s):
        x1_ref[rows, :] = x1
        hs.append((_rms(x1, g_ref[2:3, :]) * (1.0 + mod_ref[4:5, :]) + mod_ref[3:4, :]).astype(BF16))
    us = [jnp.dot(h, wx_ref[...], preferred_element_type=F32) for h in hs]
    for rows, u in zip(tiles, us):
        ua_ref[rows, :] = u[:, COL_UA:COL_UA + 512]
        ub_ref[rows, :] = u[:, COL_UB:COL_UB + 256].astype(BF16)
        qn_ref[rows, :] = (u[:, COL_QN:COL_QN + 256] * Q_SCALE).astype(BF16)
        kn_ref[rows, :] = u[:, COL_KN:COL_KN + 256].astype(BF16)
        vn_ref[rows, :] = u[:, COL_VN:COL_VN + 256].astype(BF16)
        qs = u[:, COL_QS:COL_QS + 256]
        ks = u[:, COL_KS:COL_KS + 256]
        if rope:
            cos, sin = cos_ref[rows, :], sin_ref[rows, :]
            qs = qs * cos + u[:, COL_QS_SW:COL_QS_SW + 256] * sin
            ks = ks * cos + u[:, COL_KS_SW:COL_KS_SW + 256] * sin
        qs_ref[rows, :] = (qs * Q_SCALE).astype(BF16)
        ks_ref[rows, :] = ks.astype(BF16)
        vs_ref[rows, :] = u[:, COL_VS:COL_VS + 256].astype(BF16)


def _mod_spec(layer, row0, tiles_per_batch):
    return pl.BlockSpec((None, None, N_MOD, D_MODEL), lambda i: (layer, row0 + i // tiles_per_batch, 0, 0))


def _front(x, mod, mod_row0, rows_per_batch, layer, g, w1, w3, w2, wx, rope_tabs):
    rows, d = x.shape
    tm = ROW_TILE
    tiles_per_batch = rows_per_batch // tm
    rope = rope_tabs is not None
    n_ext = N_EXT if rope else N_EXT_CTX
    row_tile = lambda w: pl.BlockSpec((tm, w), lambda i: (i, 0))
    in_specs = [row_tile(d), _mod_spec(layer, mod_row0, tiles_per_batch),
                _resident(g.shape[1:], (layer,)), _resident(w1.shape[2:], (layer, 0)),
                _resident(w3.shape[2:], (layer, 0)), _resident(w2.shape[2:], (layer, 0)),
                _resident((d, n_ext), (layer,))]
    args = [x, mod, g, w1, w3, w2, wx]
    if rope:
        tab = pl.BlockSpec((tm, 256), lambda i: (i % tiles_per_batch, 0))
        in_specs += [tab, tab]
        args += list(rope_tabs)
    out_shape = ([jax.ShapeDtypeStruct((rows, d), F32), jax.ShapeDtypeStruct((rows, 512), F32)]
                 + [jax.ShapeDtypeStruct((rows, 256), BF16)] * 7)
    out_specs = [row_tile(d), row_tile(512)] + [row_tile(256)] * 7
    return pl.pallas_call(
        functools.partial(_front_kernel, rope=rope, sub_tiles=SUB_TILES),
        grid=(rows // tm,), in_specs=in_specs, out_specs=out_specs, out_shape=out_shape,
        compiler_params=_params(1), name="front_rope" if rope else "front_ctx",
    )(*args)


def _back_kernel(x_ref, ya_ref, yb_ref, yc_ref, yd_ref, mod_ref, g_ref, wo_ref, w1_ref, w3_ref, w2_ref, o_ref, *,
                 sub_tiles):
    ts = x_ref.shape[0] // sub_tiles
    tiles = [slice(t * ts, (t + 1) * ts) for t in range(sub_tiles)]
    x2s = []
    for rows in tiles:
        y = None
        for j, y_ref in enumerate((ya_ref, yb_ref, yc_ref, yd_ref)):
            yj = jnp.dot(y_ref[rows, :], wo_ref[j * GROUP_W:(j + 1) * GROUP_W, :], preferred_element_type=F32)
            y = yj if y is None else y + yj
        x2s.append(x_ref[rows, :] + mod_ref[5:6, :] * _rms(y, g_ref[3:4, :]))
    outs = _ffn(x2s, mod_ref[6:7, :], mod_ref[7:8, :], mod_ref[8:9, :],
                g_ref[4:5, :], g_ref[5:6, :], w1_ref, w3_ref, w2_ref)
    for rows, out in zip(tiles, outs):
        o_ref[rows, :] = out


def _back(x, ys, mod, mod_row0, rows_per_batch, layer, g, wo, w1, w3, w2):
    rows, d = x.shape
    tm = ROW_TILE
    tiles_per_batch = rows_per_batch // tm
    row_tile = lambda w: pl.BlockSpec((tm, w), lambda i: (i, 0))
    in_specs = ([row_tile(d)] + [row_tile(GROUP_W)] * 4
                + [_mod_spec(layer, mod_row0, tiles_per_batch),
                   _resident(g.shape[1:], (layer,)), _resident(wo.shape[1:], (layer,)),
                   _resident(w1.shape[2:], (layer, 1)), _resident(w3.shape[2:], (layer, 1)),
                   _resident(w2.shape[2:], (layer, 1))])
    return pl.pallas_call(
        functools.partial(_back_kernel, sub_tiles=SUB_TILES), grid=(rows // tm,), in_specs=in_specs,
        out_specs=row_tile(d), out_shape=jax.ShapeDtypeStruct((rows, d), F32),
        compiler_params=_params(1), name="back",
    )(x, *ys, mod, g, wo, w1, w3, w2)


def _conv_kernel(prev_ref, cur_ref, next_ref, w_ref, p_ref, o_ref, pad_ref, *, n_chunks, rc, sub):
    c = pl.program_id(1)

    def glu(u):
        return u[:, :GROUP_W] * _sigmoid(u[:, GROUP_W:])

    pad_ref[0:CONV_HALO, :] = jnp.where(c > 0, glu(prev_ref[...]), 0.0)
    pad_ref[CONV_HALO:CONV_HALO + rc, :] = glu(cur_ref[...])
    pad_ref[CONV_HALO + rc:, :] = jnp.where(c < n_chunks - 1, glu(next_ref[...]), 0.0)
    bias, ln_g, ln_b = p_ref[0:1, :], p_ref[1:2, :], p_ref[2:3, :]
    lead = CONV_HALO - CONV_K // 2
    zrows = sub + V7X_SUBLANES
    for s in range(rc // sub):
        acc = jnp.zeros((sub, GROUP_W), F32) + bias
        for b in range(V7X_SUBLANES):
            z = None
            for a in range((lead + CONV_K - 1) // V7X_SUBLANES + 1):
                k = V7X_SUBLANES * a + b - lead
                if 0 <= k < CONV_K:
                    r0 = s * sub + V7X_SUBLANES * a
                    term = pad_ref[r0:r0 + zrows, :] * w_ref[k:k + 1, :]
                    z = term if z is None else z + term
            acc = acc + z[b:b + sub, :]
        mu = jnp.mean(acc, axis=-1, keepdims=True)
        cen = acc - mu
        var = jnp.mean(cen * cen, axis=-1, keepdims=True)
        yn = cen * lax.rsqrt(var + EPS) * ln_g + ln_b
        o_ref[s * sub:(s + 1) * sub, :] = (yn * _sigmoid(yn)).astype(BF16)


def _conv(ua, conv_w, conv_p, layer, rows_per_batch):
    rows = ua.shape[0]
    batch = rows // rows_per_batch
    rc = 256
    n_chunks = rows_per_batch // rc
    hb = rc // CONV_HALO
    n_halo = rows // CONV_HALO
    cur = pl.BlockSpec((rc, 512), lambda b, c: (b * n_chunks + c, 0))
    prev = pl.BlockSpec((CONV_HALO, 512), lambda b, c: (jnp.maximum((b * n_chunks + c) * hb - 1, 0), 0))
    nxt = pl.BlockSpec((CONV_HALO, 512), lambda b, c: (jnp.minimum((b * n_chunks + c + 1) * hb, n_halo - 1), 0))
    return pl.pallas_call(
        functools.partial(_conv_kernel, n_chunks=n_chunks, rc=rc, sub=64),
        grid=(batch, n_chunks),
        in_specs=[prev, cur, nxt, _resident(conv_w.shape[1:], (layer,)), _resident(conv_p.shape[1:], (layer,))],
        out_specs=pl.BlockSpec((rc, GROUP_W), lambda b, c: (b * n_chunks + c, 0)),
        out_shape=jax.ShapeDtypeStruct((rows, GROUP_W), BF16),
        scratch_shapes=[pltpu.VMEM((rc + 2 * CONV_HALO, GROUP_W), F32)],
        compiler_params=_params(2), name="conv",
    )(ua, ua, ua, conv_w, conv_p)


def _fourier_kernel(z_ref, wc_ref, m_ref, o_ref, *, scale):
    t = jnp.dot(z_ref[...], wc_ref[...], preferred_element_type=F32)
    zz = jnp.concatenate([t[:, :GROUP_W], t[:, GROUP_W:]], axis=0).astype(BF16)
    y = jnp.dot(m_ref[...], zz, preferred_element_type=F32)
    o_ref[...] = (y * scale).astype(BF16)


def _unit_circle(n, period):
    i = jnp.arange(n, dtype=jnp.int32)[:, None]
    j = jnp.arange(period, dtype=jnp.int32)[None, :]
    ang = (2.0 * np.pi / period) * ((i * j) % period).astype(F32)
    return jnp.cos(ang), jnp.sin(ang)


def _dft_tables(n):
    hi = max(n // GRID_W, 1)
    lo = n // hi
    k = jnp.arange(n, dtype=jnp.int32)[None, :]
    n1 = jnp.arange(hi, dtype=jnp.int32)[:, None]
    n0 = jnp.arange(lo, dtype=jnp.int32)[:, None]
    a1 = (2.0 * np.pi / hi) * ((n1 * k) % hi).astype(F32)
    a0 = (2.0 * np.pi / n) * ((n0 * k) % n).astype(F32)
    c1, s1, c0, s0 = jnp.cos(a1)[:, None], jnp.sin(a1)[:, None], jnp.cos(a0)[None], jnp.sin(a0)[None]
    cos = (c1 * c0 - s1 * s0).reshape(n, n)
    sin = (s1 * c0 + c1 * s0).reshape(n, n)
    m = jnp.concatenate([cos, -sin], axis=1).astype(BF16)
    cc, sc = _unit_circle(HEAD_DIM, HEAD_DIM)
    eye = jnp.eye(GROUP_W // HEAD_DIM, dtype=F32)
    wc = jnp.concatenate([jnp.kron(eye, cc), jnp.kron(eye, sc)], axis=1).astype(BF16)
    return wc, m


def _fourier(ub, wc, m, rows_per_batch):
    rows = ub.shape[0]
    n = rows_per_batch
    blk = pl.BlockSpec((n, GROUP_W), lambda b: (b, 0))
    return pl.pallas_call(
        functools.partial(_fourier_kernel, scale=float((n * HEAD_DIM) ** -0.5)),
        grid=(rows // n,), in_specs=[blk, _resident(wc.shape), _resident(m.shape)], out_specs=blk,
        out_shape=jax.ShapeDtypeStruct((rows, GROUP_W), BF16),
        compiler_params=_params(1), name="fourier",
    )(ub, wc, m)


def _head_mask(rows_per_head, heads=N_HEADS, head0=0):
    shape = (heads * rows_per_head, GROUP_W)
    row_h = lax.broadcasted_iota(jnp.int32, shape, 0) // rows_per_head + head0
    lane_h = lax.broadcasted_iota(jnp.int32, shape, 1) // HEAD_DIM
    return (row_h == lane_h).astype(F32)


def _stack_heads(q, mask_bf16):
    heads = mask_bf16.shape[0] // q.shape[0]
    return jnp.concatenate([q] * heads, axis=0) * mask_bf16


def _unstack_heads(o, mask_f32, rows_per_head):
    om = o * mask_f32
    out = om[0:rows_per_head]
    for h in range(1, o.shape[0] // rows_per_head):
        out = out + om[h * rows_per_head:(h + 1) * rows_per_head]
    return out


def _dot_nt(a, b):
    return lax.dot_general(a, b, (((1,), (1,)), ((), ())), preferred_element_type=F32)


def _tree(op, xs):
    while len(xs) > 1:
        xs = [op(xs[i], xs[i + 1]) if i + 1 < len(xs) else xs[i] for i in range(0, len(xs), 2)]
    return xs[0]


def _lane_tiles(*arrays):
    return [a[:, i:i + V7X_LANES] for a in arrays for i in range(0, a.shape[1], V7X_LANES)]


def _ctx_dense(cq_ref, kx, vx, sink_ref, oc_ref):
    hpc = 2
    out = None
    for g in range(N_HEADS // hpc):
        mask = _head_mask(CTX_LEN, hpc, g * hpc)
        qst = _stack_heads(cq_ref[...], mask.astype(BF16))
        s = _dot_nt(qst, kx)
        m = jnp.max(_tree(jnp.maximum, _lane_tiles(s)), axis=-1, keepdims=True)
        sink = None if sink_ref is None else sink_ref[g * hpc * CTX_LEN:(g + 1) * hpc * CTX_LEN, :]
        if sink is not None:
            m = jnp.maximum(m, sink)
        p = jnp.exp2(s - m)
        den = jnp.sum(_tree(jnp.add, _lane_tiles(p)), axis=-1, keepdims=True)
        if sink is not None:
            den = den + jnp.exp2(sink - m)
        o = jnp.dot(p.astype(BF16), vx, preferred_element_type=F32) / den
        og = _unstack_heads(o, mask, CTX_LEN)
        out = og if out is None else out + og
    oc_ref[...] = out.astype(BF16)


NAT_ROWS_PER_STEP = 2


def _pipelined_loop(n_steps, scores_fn, softmax_fn, pv_fn):
    scores_fn(0, 0)

    def pair(it, carry):
        for slot in (0, 1):
            step = 2 * it + slot
            scores_fn(jnp.minimum(step + 1, n_steps - 1), 1 - slot)
            softmax_fn(slot)
            pv_fn(step, slot)
        return carry

    lax.fori_loop(0, n_steps // 2, pair, 0)


def _pipeline_scratch(chains, rows, n_loc):
    kinds = [((chains, rows, n_loc), F32), ((chains, rows, CTX_LEN), F32),
             ((chains, rows, n_loc), BF16), ((chains, rows, CTX_LEN), BF16), ((chains, rows, 1), F32)]
    return tuple(pltpu.VMEM(shape, dtype) for shape, dtype in kinds for _ in range(2))


def _softmax_stage(scratch, slot, chain, sink):
    sl, sc, pl_, pc, dn = (scratch[2 * i + slot] for i in range(5))
    s, c = sl[chain], sc[chain]
    m = jnp.max(_tree(jnp.maximum, _lane_tiles(s, c)), axis=-1, keepdims=True)
    if sink is not None:
        m = jnp.maximum(m, sink)
    p = jnp.exp2(s - m)
    q = jnp.exp2(c - m)
    den = jnp.sum(_tree(jnp.add, _lane_tiles(p, q)), axis=-1, keepdims=True)
    if sink is not None:
        den = den + jnp.exp2(sink - m)
    pl_[chain] = p.astype(BF16)
    pc[chain] = q.astype(BF16)
    dn[chain] = den


def _pv_stage(scratch, slot, chain, vb, vx):
    pl_, pc, dn = (scratch[2 * i + slot] for i in (2, 3, 4))
    o = (jnp.dot(pl_[chain], vb, preferred_element_type=F32) + jnp.dot(pc[chain], vx, preferred_element_type=F32))
    return o / dn[chain]


def _nat_kernel(*refs, ctx_out):
    n_io = 9 if ctx_out else 7
    scratch = refs[n_io:]
    if ctx_out:
        q_ref, k_ref, v_ref, kx_ref, vx_ref, bias_ref, cq_ref, o_ref, oc_ref = refs[:n_io]
    else:
        q_ref, k_ref, v_ref, kx_ref, vx_ref, bias_ref, o_ref = refs[:n_io]
    mask = _head_mask(GRID_W)
    mask_b = mask.astype(BF16)
    band = NAT_ROWS * GRID_W
    rps = NAT_ROWS_PER_STEP

    def geometry(r):
        rs = jnp.clip(r - NAT_ROWS // 2, 0, GRID_H - NAT_ROWS)
        return pl.multiple_of(r * GRID_W, GRID_W), pl.multiple_of(rs * GRID_W, GRID_W), rs - r + NAT_ROWS - 1

    def scores(step, slot):
        for j in range(rps):
            q0, k0, d0 = geometry(step * rps + j)
            qst = _stack_heads(q_ref[pl.ds(q0, GRID_W), :], mask_b)
            bias = jnp.concatenate([bias_ref[d0 + 2 * a] for a in range(NAT_ROWS // 2)], axis=1)
            scratch[slot][j] = _dot_nt(qst, k_ref[pl.ds(k0, band), :]) + bias
            scratch[2 + slot][j] = _dot_nt(qst, kx_ref[...])

    def softmax(slot):
        for j in range(rps):
            _softmax_stage(scratch, slot, j, None)

    def pv(step, slot):
        for j in range(rps):
            q0, k0, _ = geometry(step * rps + j)
            o = _pv_stage(scratch, slot, j, v_ref[pl.ds(k0, band), :], vx_ref[...])
            o_ref[pl.ds(q0, GRID_W), :] = _unstack_heads(o, mask, GRID_W).astype(BF16)

    _pipelined_loop(GRID_H // rps, scores, softmax, pv)
    if ctx_out:
        _ctx_dense(cq_ref, kx_ref[...], vx_ref[...], None, oc_ref)


def _nat_bias_table(rel_bias):
    depth = rel_bias.shape[0]
    w = GRID_W
    n_dr = 2 * NAT_ROWS - 1
    lo = w - NAT_COLS
    g = jnp.pad(rel_bias.astype(F32) * LOG2_E, ((0, 0),) * 3 + ((lo, 2 * w - lo - (2 * NAT_COLS - 1)),))
    flat = jnp.tile(g, (1, 1, 1, w))[..., :w * (2 * w - 1)]
    t = flat.reshape(depth, N_HEADS, n_dr, w, 2 * w - 1)[..., w - 1:]
    qc = np.arange(w)[:, None]
    kc = np.arange(w)[None, :]
    ws = np.clip(qc - NAT_COLS // 2, 0, w - NAT_COLS)
    col_ok = (kc >= ws) & (kc < ws + NAT_COLS)
    t = jnp.where(col_ok, t, NEG).transpose(0, 2, 1, 3, 4).reshape(depth, n_dr, N_HEADS * w, w)
    return jnp.concatenate([t[:, :-1], t[:, 1:]], axis=-1)


def _attention_call(kernel, name, lat, ctx, consts, ctx_out, scratch=()):
    batch = lat[0].shape[0] // SEQ
    lat_blk = pl.BlockSpec((SEQ, GROUP_W), lambda b: (b, 0))
    ctx_blk = pl.BlockSpec((CTX_LEN, GROUP_W), lambda b: (b, 0))
    in_specs = [lat_blk] * 3 + [ctx_blk] * 2 + [_resident(c.shape[len(lead):], lead) for c, lead in consts]
    args = list(lat) + [ctx[1], ctx[2]] + [c for c, _ in consts]
    out_shape = [jax.ShapeDtypeStruct((batch * SEQ, GROUP_W), BF16)]
    out_specs = [lat_blk]
    if ctx_out:
        in_specs.append(ctx_blk)
        args.append(ctx[0])
        out_shape.append(jax.ShapeDtypeStruct((batch * CTX_LEN, GROUP_W), BF16))
        out_specs.append(ctx_blk)
    outs = pl.pallas_call(
        functools.partial(kernel, ctx_out=ctx_out), grid=(batch,), in_specs=in_specs,
        out_specs=out_specs, out_shape=out_shape, scratch_shapes=list(scratch),
        compiler_params=_params(1), name=name,
    )(*args)
    return outs if ctx_out else (outs[0], None)


SWA_HEADS_PER_CHAIN = 2


def _swa_kernel(*refs, ctx_out):
    n_io = 11 if ctx_out else 9
    scratch = refs[n_io:]
    if ctx_out:
        q_ref, k_ref, v_ref, kx_ref, vx_ref, sink_ref, csink_ref, cq_ref, o_ref, oc_ref, win_ref = refs[:n_io]
    else:
        q_ref, k_ref, v_ref, kx_ref, vx_ref, sink_ref, csink_ref, o_ref, win_ref = refs[:n_io]
    hpc = SWA_HEADS_PER_CHAIN
    chain_rows = hpc * SWA_BLOCK
    band = 3 * SWA_BLOCK
    n_blocks = SEQ // SWA_BLOCK

    @pl.when(pl.program_id(0) == 0)
    def _():
        shape = (chain_rows, band)
        rel = (lax.broadcasted_iota(jnp.int32, shape, 1)
               - lax.broadcasted_iota(jnp.int32, shape, 0) % SWA_BLOCK)
        for i, shift in enumerate((0, -SWA_BLOCK, -2 * SWA_BLOCK)):
            win_ref[i] = jnp.where(jnp.abs(rel + shift) <= SWA_WINDOW, 0.0, NEG)

    masks = [_head_mask(SWA_BLOCK, hpc, g * hpc) for g in range(N_HEADS // hpc)]
    masks_b = [mask.astype(BF16) for mask in masks]

    def geometry(nb):
        q0 = pl.multiple_of(nb * SWA_BLOCK, SWA_BLOCK)
        return q0, pl.multiple_of(jnp.clip(q0 - SWA_BLOCK, 0, SEQ - band), SWA_BLOCK)

    def scores(nb, slot):
        q0, k0 = geometry(nb)
        placement = jnp.where(nb == 0, 0, jnp.where(nb == n_blocks - 1, 2, 1))
        q = q_ref[pl.ds(q0, SWA_BLOCK), :]
        for g in range(len(masks)):
            qst = _stack_heads(q, masks_b[g])
            scratch[slot][g] = _dot_nt(qst, k_ref[pl.ds(k0, band), :]) + win_ref[placement]
            scratch[2 + slot][g] = _dot_nt(qst, kx_ref[...])

    def softmax(slot):
        for g in range(len(masks)):
            _softmax_stage(scratch, slot, g, sink_ref[g * chain_rows:(g + 1) * chain_rows, :])

    def pv(nb, slot):
        q0, k0 = geometry(nb)
        out = None
        for g, mask in enumerate(masks):
            o = _pv_stage(scratch, slot, g, v_ref[pl.ds(k0, band), :], vx_ref[...])
            og = _unstack_heads(o, mask, SWA_BLOCK)
            out = og if out is None else out + og
        o_ref[pl.ds(q0, SWA_BLOCK), :] = out.astype(BF16)

    _pipelined_loop(n_blocks, scores, softmax, pv)
    if ctx_out:
        _ctx_dense(cq_ref, kx_ref[...], vx_ref[...], csink_ref, oc_ref)


def _rope_tables():
    t = np.arange(SEQ)
    pos = np.stack([t // GRID_W, t % GRID_W], axis=1).astype(np.float32)
    half = HEAD_DIM // 4
    inv = ROPE_BASE ** (-jnp.arange(half, dtype=F32) / half)
    ang = jnp.asarray(pos)[:, :, None] * inv[None, None, :]
    cos = jnp.cos(ang)
    sin = jnp.sin(ang)
    cos64 = jnp.concatenate([cos, cos], axis=-1).reshape(SEQ, HEAD_DIM)
    sin64 = jnp.concatenate([-sin, sin], axis=-1).reshape(SEQ, HEAD_DIM)
    return jnp.tile(cos64, (1, N_HEADS)), jnp.tile(sin64, (1, N_HEADS))


def _extended_w_in(w_in):
    lead = w_in.shape[:-1]
    group = N_HEADS // SWA_KV_HEADS

    def per_query_head(w):
        return jnp.repeat(w.reshape(lead + (SWA_KV_HEADS, HEAD_DIM)), group, axis=-2).reshape(lead + (GROUP_W,))

    def rotate_half_partner(w):
        return w.reshape(lead + (N_HEADS, 2, 2, HEAD_DIM // 4))[..., ::-1, :].reshape(lead + (GROUP_W,))

    qs = w_in[..., 1536:1792]
    ks = per_query_head(w_in[..., 1792:1920])
    vs = per_query_head(w_in[..., 1920:2048])
    return jnp.concatenate([w_in[..., :1792], ks, vs, rotate_half_partner(qs), rotate_half_partner(ks)],
                           axis=-1).astype(BF16)


def kernel(x, c, ctx, c_ctx, w_ada, b_ada, norm_g, ffn_w1, ffn_w3, ffn_w2, w_in, conv_w, conv_b, conv_ln_g,
           conv_ln_b, nat_rel_bias, sink_logits, w_out):
    bsz, seq, d = x.shape
    depth = w_ada.shape[0]
    assert (seq, d, ctx.shape[1]) == (SEQ, D_MODEL, CTX_LEN) and bsz < MOD_ROWS
    xl = x.reshape(bsz * seq, d)
    xc = ctx.reshape(bsz * CTX_LEN, d)

    cv = jnp.zeros((MOD_ROWS, d), F32).at[:bsz].set(c).at[bsz].set(c_ctx)
    mod = _ada_mod(cv, w_ada, b_ada).reshape(depth, MOD_ROWS, N_MOD, d)
    rope_tabs = _rope_tables()
    dft_lat = _dft_tables(SEQ)
    dft_ctx = _dft_tables(CTX_LEN)
    w1, w3, w2 = ffn_w1.astype(BF16), ffn_w3.astype(BF16), ffn_w2.astype(BF16)
    wx = _extended_w_in(w_in)
    wo = w_out.astype(BF16)
    conv_p = jnp.stack([conv_b, conv_ln_g, conv_ln_b], axis=1)
    nat_bias = _nat_bias_table(nat_rel_bias)
    sink_lat = jnp.repeat(sink_logits * LOG2_E, SWA_BLOCK, axis=1)[..., None]
    sink_ctx = jnp.repeat(sink_logits * LOG2_E, CTX_LEN, axis=1)[..., None]
    swa_rows, swa_chains = SWA_HEADS_PER_CHAIN * SWA_BLOCK, N_HEADS // SWA_HEADS_PER_CHAIN
    swa_scratch = ((pltpu.VMEM((3, swa_rows, 3 * SWA_BLOCK), F32),)
                   + _pipeline_scratch(swa_chains, swa_rows, 3 * SWA_BLOCK))
    nat_scratch = _pipeline_scratch(NAT_ROWS_PER_STEP, N_HEADS * GRID_W, NAT_ROWS * GRID_W)

    for l in range(depth):
        last = l == depth - 1
        xl1, ua, ub, qn, kn, vn, qs, ks, vs = _front(xl, mod, 0, SEQ, l, norm_g, w1, w3, w2, wx, rope_tabs)
        xc1, ca, cb, cqn, ckn, cvn, cqs, cks, cvs = _front(xc, mod, bsz, bsz * CTX_LEN, l, norm_g, w1, w3, w2,
                                                           wx, None)
        y_a = _conv(ua, conv_w, conv_p, l, SEQ)
        y_b = _fourier(ub, *dft_lat, SEQ)
        y_c, yc_c = _attention_call(_nat_kernel, "nat", (qn, kn, vn), (cqn, ckn, cvn), ((nat_bias, (l,)),),
                                    not last, scratch=nat_scratch)
        y_d, yc_d = _attention_call(_swa_kernel, "swa", (qs, ks, vs), (cqs, cks, cvs),
                                    ((sink_lat, (l,)), (sink_ctx, (l,))), not last, scratch=swa_scratch)
        xl = _back(xl1, (y_a, y_b, y_c, y_d), mod, 0, SEQ, l, norm_g, wo, w1, w3, w2)
        if not last:
            yc_a = _conv(ca, conv_w, conv_p, l, CTX_LEN)
            yc_b = _fourier(cb, *dft_ctx, CTX_LEN)
            xc = _back(xc1, (yc_a, yc_b, yc_c, yc_d), mod, bsz, bsz * CTX_LEN, l, norm_g, wo, w1, w3, w2)
    return xl.reshape(bsz, seq, d)
```

```python
import functools

import numpy as np
import jax
import jax.numpy as jnp
from jax import lax
from jax.experimental import pallas as pl
from jax.experimental.pallas import tpu as pltpu

F32 = jnp.float32
BF16 = jnp.bfloat16

D_MODEL = 1024
SEQ = 2048
CTX_LEN = 256
GRID_W = 64
GRID_H = SEQ // GRID_W
GROUP_W = 256
HEAD_DIM = 64
N_HEADS = 4
CONV_K = 31
NAT_ROWS = 8
NAT_COLS = 16
SWA_KV_HEADS = 2
SWA_WINDOW = 128
SWA_BLOCK = 128
ROPE_BASE = 10000.0
FFN_DIM = 2816
MACARON_W = 0.5
N_MOD = 9
MOD_ROWS = 24
EPS = 1e-6
NEG = -1e30
LOG2_E = float(np.log2(np.e))
Q_SCALE = HEAD_DIM ** -0.5 * LOG2_E

V7X_LANES = 128
V7X_SUBLANES = 8
V7X_MXU_DIM = 256
V7X_VMEM_LIMIT = 56 * 1024 * 1024
ROW_TILE = 512
SUB_TILES = 2
FFN_CHUNK_BOUNDS = (0, 6 * V7X_MXU_DIM, FFN_DIM)
CONV_HALO = 2 * V7X_SUBLANES

COL_UA = 0
COL_UB = 512
COL_QN, COL_KN, COL_VN = 768, 1024, 1280
COL_QS, COL_KS, COL_VS = 1536, 1792, 1920
IN_DIM = 2048


def _resident(shape, lead=()):
    nd = len(shape)
    return pl.BlockSpec((None,) * len(lead) + tuple(shape), lambda *_: tuple(lead) + (0,) * nd,
                        pipeline_mode=pl.Buffered(1))


def _params(n_axes=1):
    return pltpu.CompilerParams(dimension_semantics=("arbitrary",) * n_axes, vmem_limit_bytes=V7X_VMEM_LIMIT)


def _rms(x, g):
    return x * lax.rsqrt(jnp.mean(x * x, axis=-1, keepdims=True) + EPS) * g


def _sigmoid(x):
    return 1.0 / (1.0 + jnp.exp(-x))


def _ffn(xs, shift, scale, gate, g_pre, g_post, w1_ref, w3_ref, w2_ref):
    hs = [(_rms(x, g_pre) * (1.0 + scale) + shift).astype(BF16) for x in xs]
    ys = [None] * len(xs)
    for lo, hi in zip(FFN_CHUNK_BOUNDS[:-1], FFN_CHUNK_BOUNDS[1:]):
        ts = []
        for h in hs:
            a = jnp.dot(h, w1_ref[:, lo:hi], preferred_element_type=F32)
            b = jnp.dot(h, w3_ref[:, lo:hi], preferred_element_type=F32)
            ts.append((a * _sigmoid(a) * b).astype(BF16))
        for i, t in enumerate(ts):
            yc = jnp.dot(t, w2_ref[lo:hi, :], preferred_element_type=F32)
            ys[i] = yc if ys[i] is None else ys[i] + yc
    return [x + (MACARON_W * gate) * _rms(y, g_post) for x, y in zip(xs, ys)]


def _ada_kernel(cv_ref, w_ref, b_ref, o_ref):
    c = cv_ref[...]
    sc = (c * _sigmoid(c)).astype(BF16)
    o_ref[...] = jnp.dot(sc, w_ref[...].astype(BF16), preferred_element_type=F32) + b_ref[...]


def _ada_mod(cv, w_ada, b_ada):
    depth, d, n = w_ada.shape
    rows = cv.shape[0]
    tn = 9 * V7X_LANES
    return pl.pallas_call(
        _ada_kernel,
        grid=(depth, n // tn),
        in_specs=[pl.BlockSpec((rows, d), lambda l, j: (0, 0)),
                  pl.BlockSpec((None, d, tn), lambda l, j: (l, 0, j)),
                  pl.BlockSpec((None, 1, tn), lambda l, j: (l, 0, j))],
        out_specs=pl.BlockSpec((None, rows, tn), lambda l, j: (l, 0, j)),
        out_shape=jax.ShapeDtypeStruct((depth, rows, n), F32),
        compiler_params=_params(2),
        name="ada_mod",
    )(cv, w_ada, b_ada.reshape(depth, 1, n))


def _lane_index(shape):
    return lax.broadcasted_iota(jnp.int32, shape, 1)


def _rotate_half_partner(t):
    quarter = HEAD_DIM // 4
    up = pltpu.roll(t, V7X_LANES - quarter, axis=1)
    down = pltpu.roll(t, quarter, axis=1)
    return jnp.where(_lane_index(t.shape) % (2 * quarter) < quarter, up, down)


def _per_query_head(t):
    swapped = pltpu.roll(t, HEAD_DIM, axis=1)
    first = _lane_index(t.shape) < HEAD_DIM
    return jnp.concatenate([jnp.where(first, t, swapped), jnp.where(first, swapped, t)], axis=1)


def _front_kernel(*refs, rope, sub_tiles):
    if rope:
        (x_ref, mod_ref, g_ref, w1_ref, w3_ref, w2_ref, wx_ref, cos_ref, sin_ref,
         x1_ref, ua_ref, ub_ref, qn_ref, kn_ref, vn_ref, qs_ref, ks_ref, vs_ref) = refs
    else:
        (x_ref, mod_ref, g_ref, w1_ref, w3_ref, w2_ref, wx_ref,
         x1_ref, ua_ref, ub_ref, qn_ref, kn_ref, vn_ref, qs_ref, ks_ref, vs_ref) = refs
    ts = x_ref.shape[0] // sub_tiles
    tiles = [slice(t * ts, (t + 1) * ts) for t in range(sub_tiles)]
    x1s = _ffn([x_ref[rows, :] for rows in tiles], mod_ref[0:1, :], mod_ref[1:2, :], mod_ref[2:3, :],
               g_ref[0:1, :], g_ref[1:2, :], w1_ref, w3_ref, w2_ref)
    hs = []
    for rows, x1 in zip(tiles, x1s):
        x1_ref[rows, :] = x1
        hs.append((_rms(x1, g_ref[2:3, :]) * (1.0 + mod_ref[4:5, :]) + mod_ref[3:4, :]).astype(BF16))
    us = [jnp.dot(h, wx_ref[...], preferred_element_type=F32) for h in hs]
    for rows, u in zip(tiles, us):
        ua_ref[rows, :] = u[:, COL_UA:COL_UA + 512]
        ub_ref[rows, :] = u[:, COL_UB:COL_UB + 256].astype(BF16)
        qn_ref[rows, :] = (u[:, COL_QN:COL_QN + 256] * Q_SCALE).astype(BF16)
        kn_ref[rows, :] = u[:, COL_KN:COL_KN + 256].astype(BF16)
        vn_ref[rows, :] = u[:, COL_VN:COL_VN + 256].astype(BF16)
        q_tiles = [u[:, COL_QS + i * V7X_LANES:COL_QS + (i + 1) * V7X_LANES] for i in range(2)]
        ks = u[:, COL_KS:COL_KS + V7X_LANES]
        if rope:
            cos, sin = cos_ref[rows, :], sin_ref[rows, :]
            q_tiles = [t * cos + _rotate_half_partner(t) * sin for t in q_tiles]
            ks = ks * cos + _rotate_half_partner(ks) * sin
        qs_ref[rows, :] = (jnp.concatenate(q_tiles, axis=1) * Q_SCALE).astype(BF16)
        ks_ref[rows, :] = _per_query_head(ks).astype(BF16)
        vs_ref[rows, :] = _per_query_head(u[:, COL_VS:COL_VS + V7X_LANES]).astype(BF16)


def _mod_spec(layer, row0, tiles_per_batch):
    return pl.BlockSpec((None, None, N_MOD, D_MODEL), lambda i: (layer, row0 + i // tiles_per_batch, 0, 0))


def _front(x, mod, mod_row0, rows_per_batch, layer, g, w1, w3, w2, wx, rope_tabs):
    rows, d = x.shape
    tm = ROW_TILE
    tiles_per_batch = rows_per_batch // tm
    rope = rope_tabs is not None
    row_tile = lambda w: pl.BlockSpec((tm, w), lambda i: (i, 0))
    in_specs = [row_tile(d), _mod_spec(layer, mod_row0, tiles_per_batch),
                _resident(g.shape[1:], (layer,)), _resident(w1.shape[2:], (layer, 0)),
                _resident(w3.shape[2:], (layer, 0)), _resident(w2.shape[2:], (layer, 0)),
                _resident(wx.shape[1:], (layer,))]
    args = [x, mod, g, w1, w3, w2, wx]
    if rope:
        tab = pl.BlockSpec((tm, V7X_LANES), lambda i: (i % tiles_per_batch, 0))
        in_specs += [tab, tab]
        args += list(rope_tabs)
    out_shape = ([jax.ShapeDtypeStruct((rows, d), F32), jax.ShapeDtypeStruct((rows, 512), F32)]
                 + [jax.ShapeDtypeStruct((rows, 256), BF16)] * 7)
    out_specs = [row_tile(d), row_tile(512)] + [row_tile(256)] * 7
    return pl.pallas_call(
        functools.partial(_front_kernel, rope=rope, sub_tiles=SUB_TILES),
        grid=(rows // tm,), in_specs=in_specs, out_specs=out_specs, out_shape=out_shape,
        compiler_params=_params(1), name="front_rope" if rope else "front_ctx",
    )(*args)


def _back_kernel(x_ref, ya_ref, yb_ref, yc_ref, yd_ref, mod_ref, g_ref, wo_ref, w1_ref, w3_ref, w2_ref, o_ref, *,
                 sub_tiles):
    ts = x_ref.shape[0] // sub_tiles
    tiles = [slice(t * ts, (t + 1) * ts) for t in range(sub_tiles)]
    x2s = []
    for rows in tiles:
        y = None
        for j, y_ref in enumerate((ya_ref, yb_ref, yc_ref, yd_ref)):
            yj = jnp.dot(y_ref[rows, :], wo_ref[j * GROUP_W:(j + 1) * GROUP_W, :], preferred_element_type=F32)
            y = yj if y is None else y + yj
        x2s.append(x_ref[rows, :] + mod_ref[5:6, :] * _rms(y, g_ref[3:4, :]))
    outs = _ffn(x2s, mod_ref[6:7, :], mod_ref[7:8, :], mod_ref[8:9, :],
                g_ref[4:5, :], g_ref[5:6, :], w1_ref, w3_ref, w2_ref)
    for rows, out in zip(tiles, outs):
        o_ref[rows, :] = out


def _back(x, ys, mod, mod_row0, rows_per_batch, layer, g, wo, w1, w3, w2):
    rows, d = x.shape
    tm = ROW_TILE
    tiles_per_batch = rows_per_batch // tm
    row_tile = lambda w: pl.BlockSpec((tm, w), lambda i: (i, 0))
    in_specs = ([row_tile(d)] + [row_tile(GROUP_W)] * 4
                + [_mod_spec(layer, mod_row0, tiles_per_batch),
                   _resident(g.shape[1:], (layer,)), _resident(wo.shape[1:], (layer,)),
                   _resident(w1.shape[2:], (layer, 1)), _resident(w3.shape[2:], (layer, 1)),
                   _resident(w2.shape[2:], (layer, 1))])
    return pl.pallas_call(
        functools.partial(_back_kernel, sub_tiles=SUB_TILES), grid=(rows // tm,), in_specs=in_specs,
        out_specs=row_tile(d), out_shape=jax.ShapeDtypeStruct((rows, d), F32),
        compiler_params=_params(1), name="back",
    )(x, *ys, mod, g, wo, w1, w3, w2)


def _conv_kernel(prev_ref, cur_ref, next_ref, w_ref, p_ref, o_ref, pad_ref, *, n_chunks, rc, sub):
    c = pl.program_id(1)

    def glu(u):
        return u[:, :GROUP_W] * _sigmoid(u[:, GROUP_W:])

    pad_ref[0:CONV_HALO, :] = jnp.where(c > 0, glu(prev_ref[...]), 0.0)
    pad_ref[CONV_HALO:CONV_HALO + rc, :] = glu(cur_ref[...])
    pad_ref[CONV_HALO + rc:, :] = jnp.where(c < n_chunks - 1, glu(next_ref[...]), 0.0)
    bias, ln_g, ln_b = p_ref[0:1, :], p_ref[1:2, :], p_ref[2:3, :]
    lead = CONV_HALO - CONV_K // 2
    zrows = sub + V7X_SUBLANES
    for s in range(rc // sub):
        acc = jnp.zeros((sub, GROUP_W), F32) + bias
        for b in range(V7X_SUBLANES):
            z = None
            for a in range((lead + CONV_K - 1) // V7X_SUBLANES + 1):
                k = V7X_SUBLANES * a + b - lead
                if 0 <= k < CONV_K:
                    r0 = s * sub + V7X_SUBLANES * a
                    term = pad_ref[r0:r0 + zrows, :] * w_ref[k:k + 1, :]
                    z = term if z is None else z + term
            acc = acc + z[b:b + sub, :]
        mu = jnp.mean(acc, axis=-1, keepdims=True)
        cen = acc - mu
        var = jnp.mean(cen * cen, axis=-1, keepdims=True)
        yn = cen * lax.rsqrt(var + EPS) * ln_g + ln_b
        o_ref[s * sub:(s + 1) * sub, :] = (yn * _sigmoid(yn)).astype(BF16)


def _conv(ua, conv_w, conv_p, layer, rows_per_batch):
    rows = ua.shape[0]
    batch = rows // rows_per_batch
    rc = 256
    n_chunks = rows_per_batch // rc
    hb = rc // CONV_HALO
    n_halo = rows // CONV_HALO
    cur = pl.BlockSpec((rc, 512), lambda b, c: (b * n_chunks + c, 0))
    prev = pl.BlockSpec((CONV_HALO, 512), lambda b, c: (jnp.maximum((b * n_chunks + c) * hb - 1, 0), 0))
    nxt = pl.BlockSpec((CONV_HALO, 512), lambda b, c: (jnp.minimum((b * n_chunks + c + 1) * hb, n_halo - 1), 0))
    return pl.pallas_call(
        functools.partial(_conv_kernel, n_chunks=n_chunks, rc=rc, sub=64),
        grid=(batch, n_chunks),
        in_specs=[prev, cur, nxt, _resident(conv_w.shape[1:], (layer,)), _resident(conv_p.shape[1:], (layer,))],
        out_specs=pl.BlockSpec((rc, GROUP_W), lambda b, c: (b * n_chunks + c, 0)),
        out_shape=jax.ShapeDtypeStruct((rows, GROUP_W), BF16),
        scratch_shapes=[pltpu.VMEM((rc + 2 * CONV_HALO, GROUP_W), F32)],
        compiler_params=_params(2), name="conv",
    )(ua, ua, ua, conv_w, conv_p)


FOURIER_BATCHES_PER_STEP = 2


def _fourier_kernel(z_ref, wc_ref, m_ref, o_ref, *, n, scale):
    seqs = [slice(i * n, (i + 1) * n) for i in range(FOURIER_BATCHES_PER_STEP)]
    ts = [jnp.dot(z_ref[rows, :], wc_ref[...], preferred_element_type=F32) for rows in seqs]
    zzs = [jnp.concatenate([t[:, :GROUP_W], t[:, GROUP_W:]], axis=0).astype(BF16) for t in ts]
    ys = [jnp.dot(m_ref[...], zz, preferred_element_type=F32) for zz in zzs]
    for rows, y in zip(seqs, ys):
        o_ref[rows, :] = (y * scale).astype(BF16)


def _unit_circle(n, period):
    i = jnp.arange(n, dtype=jnp.int32)[:, None]
    j = jnp.arange(period, dtype=jnp.int32)[None, :]
    ang = (2.0 * np.pi / period) * ((i * j) % period).astype(F32)
    return jnp.cos(ang), jnp.sin(ang)


def _dft_tables(n):
    hi = max(n // GRID_W, 1)
    lo = n // hi
    k = jnp.arange(n, dtype=jnp.int32)[None, :]
    n1 = jnp.arange(hi, dtype=jnp.int32)[:, None]
    n0 = jnp.arange(lo, dtype=jnp.int32)[:, None]
    a1 = (2.0 * np.pi / hi) * ((n1 * k) % hi).astype(F32)
    a0 = (2.0 * np.pi / n) * ((n0 * k) % n).astype(F32)
    c1, s1, c0, s0 = jnp.cos(a1)[:, None], jnp.sin(a1)[:, None], jnp.cos(a0)[None], jnp.sin(a0)[None]
    cos = (c1 * c0 - s1 * s0).reshape(n, n)
    sin = (s1 * c0 + c1 * s0).reshape(n, n)
    m = jnp.concatenate([cos, -sin], axis=1).astype(BF16)
    cc, sc = _unit_circle(HEAD_DIM, HEAD_DIM)
    eye = jnp.eye(GROUP_W // HEAD_DIM, dtype=F32)
    wc = jnp.concatenate([jnp.kron(eye, cc), jnp.kron(eye, sc)], axis=1).astype(BF16)
    return wc, m


def _fourier(ub, wc, m, rows_per_batch):
    rows = ub.shape[0]
    n = rows_per_batch
    step_rows = FOURIER_BATCHES_PER_STEP * n
    blk = pl.BlockSpec((step_rows, GROUP_W), lambda b: (b, 0))
    return pl.pallas_call(
        functools.partial(_fourier_kernel, n=n, scale=float((n * HEAD_DIM) ** -0.5)),
        grid=(rows // step_rows,), in_specs=[blk, _resident(wc.shape), _resident(m.shape)], out_specs=blk,
        out_shape=jax.ShapeDtypeStruct((rows, GROUP_W), BF16),
        compiler_params=_params(1), name="fourier",
    )(ub, wc, m)


def _head_mask(rows_per_head, heads=N_HEADS, head0=0):
    shape = (heads * rows_per_head, GROUP_W)
    row_h = lax.broadcasted_iota(jnp.int32, shape, 0) // rows_per_head + head0
    lane_h = lax.broadcasted_iota(jnp.int32, shape, 1) // HEAD_DIM
    return (row_h == lane_h).astype(F32)


def _stack_heads(q, mask_bf16):
    heads = mask_bf16.shape[0] // q.shape[0]
    return jnp.concatenate([q] * heads, axis=0) * mask_bf16


def _unstack_heads(o, mask_f32, rows_per_head):
    om = o * mask_f32
    out = om[0:rows_per_head]
    for h in range(1, o.shape[0] // rows_per_head):
        out = out + om[h * rows_per_head:(h + 1) * rows_per_head]
    return out


def _dot_nt(a, b):
    return lax.dot_general(a, b, (((1,), (1,)), ((), ())), preferred_element_type=F32)


def _tree(op, xs):
    while len(xs) > 1:
        xs = [op(xs[i], xs[i + 1]) if i + 1 < len(xs) else xs[i] for i in range(0, len(xs), 2)]
    return xs[0]


def _lane_tiles(*arrays):
    return [a[:, i:i + V7X_LANES] for a in arrays for i in range(0, a.shape[1], V7X_LANES)]


def _ctx_dense(cq_ref, kx, vx, sink_ref, oc_ref):
    hpc = 2
    out = None
    for g in range(N_HEADS // hpc):
        mask = _head_mask(CTX_LEN, hpc, g * hpc)
        qst = _stack_heads(cq_ref[...], mask.astype(BF16))
        s = _dot_nt(qst, kx)
        m = jnp.max(_tree(jnp.maximum, _lane_tiles(s)), axis=-1, keepdims=True)
        sink = None if sink_ref is None else sink_ref[g * hpc * CTX_LEN:(g + 1) * hpc * CTX_LEN, :]
        if sink is not None:
            m = jnp.maximum(m, sink)
        p = jnp.exp2(s - m)
        den = jnp.sum(_tree(jnp.add, _lane_tiles(p)), axis=-1, keepdims=True)
        if sink is not None:
            den = den + jnp.exp2(sink - m)
        o = jnp.dot(p.astype(BF16), vx, preferred_element_type=F32) / den
        og = _unstack_heads(o, mask, CTX_LEN)
        out = og if out is None else out + og
    oc_ref[...] = out.astype(BF16)


NAT_ROWS_PER_STEP = 2


def _pipelined_loop(n_steps, scores_fn, softmax_fn, pv_fn):
    scores_fn(0, 0)

    def pair(it, carry):
        for slot in (0, 1):
            step = 2 * it + slot
            scores_fn(jnp.minimum(step + 1, n_steps - 1), 1 - slot)
            softmax_fn(slot)
            pv_fn(step, slot)
        return carry

    lax.fori_loop(0, n_steps // 2, pair, 0)


def _pipeline_scratch(chains, rows, n_loc):
    kinds = [((chains, rows, n_loc), F32), ((chains, rows, CTX_LEN), F32),
             ((chains, rows, n_loc), BF16), ((chains, rows, CTX_LEN), BF16), ((chains, rows, 1), F32)]
    return tuple(pltpu.VMEM(shape, dtype) for shape, dtype in kinds for _ in range(2))


def _softmax_stage(scratch, slot, chain, sink):
    sl, sc, pl_, pc, dn = (scratch[2 * i + slot] for i in range(5))
    s, c = sl[chain], sc[chain]
    m = jnp.max(_tree(jnp.maximum, _lane_tiles(s, c)), axis=-1, keepdims=True)
    if sink is not None:
        m = jnp.maximum(m, sink)
    p = jnp.exp2(s - m)
    q = jnp.exp2(c - m)
    den = jnp.sum(_tree(jnp.add, _lane_tiles(p, q)), axis=-1, keepdims=True)
    if sink is not None:
        den = den + jnp.exp2(sink - m)
    pl_[chain] = p.astype(BF16)
    pc[chain] = q.astype(BF16)
    dn[chain] = den


def _pv_stage(scratch, slot, chain, vb, vx):
    pl_, pc, dn = (scratch[2 * i + slot] for i in (2, 3, 4))
    o = (jnp.dot(pl_[chain], vb, preferred_element_type=F32) + jnp.dot(pc[chain], vx, preferred_element_type=F32))
    return o / dn[chain]


def _nat_kernel(*refs, ctx_out):
    n_io = 9 if ctx_out else 7
    scratch = refs[n_io:]
    if ctx_out:
        q_ref, k_ref, v_ref, kx_ref, vx_ref, bias_ref, cq_ref, o_ref, oc_ref = refs[:n_io]
    else:
        q_ref, k_ref, v_ref, kx_ref, vx_ref, bias_ref, o_ref = refs[:n_io]
    mask = _head_mask(GRID_W)
    mask_b = mask.astype(BF16)
    band = NAT_ROWS * GRID_W
    rps = NAT_ROWS_PER_STEP

    def geometry(r):
        rs = jnp.clip(r - NAT_ROWS // 2, 0, GRID_H - NAT_ROWS)
        return pl.multiple_of(r * GRID_W, GRID_W), pl.multiple_of(rs * GRID_W, GRID_W), rs - r + NAT_ROWS - 1

    def scores(step, slot):
        for j in range(rps):
            q0, k0, d0 = geometry(step * rps + j)
            qst = _stack_heads(q_ref[pl.ds(q0, GRID_W), :], mask_b)
            bias = jnp.concatenate([bias_ref[d0 + 2 * a] for a in range(NAT_ROWS // 2)], axis=1)
            scratch[slot][j] = _dot_nt(qst, k_ref[pl.ds(k0, band), :]) + bias
            scratch[2 + slot][j] = _dot_nt(qst, kx_ref[...])

    def softmax(slot):
        for j in range(rps):
            _softmax_stage(scratch, slot, j, None)

    def pv(step, slot):
        for j in range(rps):
            q0, k0, _ = geometry(step * rps + j)
            o = _pv_stage(scratch, slot, j, v_ref[pl.ds(k0, band), :], vx_ref[...])
            o_ref[pl.ds(q0, GRID_W), :] = _unstack_heads(o, mask, GRID_W).astype(BF16)

    _pipelined_loop(GRID_H // rps, scores, softmax, pv)
    if ctx_out:
        _ctx_dense(cq_ref, kx_ref[...], vx_ref[...], None, oc_ref)


def _nat_bias_table(rel_bias):
    depth = rel_bias.shape[0]
    w = GRID_W
    n_dr = 2 * NAT_ROWS - 1
    lo = w - NAT_COLS
    g = jnp.pad(rel_bias.astype(F32) * LOG2_E, ((0, 0),) * 3 + ((lo, 2 * w - lo - (2 * NAT_COLS - 1)),))
    flat = jnp.tile(g, (1, 1, 1, w))[..., :w * (2 * w - 1)]
    t = flat.reshape(depth, N_HEADS, n_dr, w, 2 * w - 1)[..., w - 1:]
    qc = np.arange(w)[:, None]
    kc = np.arange(w)[None, :]
    ws = np.clip(qc - NAT_COLS // 2, 0, w - NAT_COLS)
    col_ok = (kc >= ws) & (kc < ws + NAT_COLS)
    t = jnp.where(col_ok, t, NEG).transpose(0, 2, 1, 3, 4).reshape(depth, n_dr, N_HEADS * w, w)
    return jnp.concatenate([t[:, :-1], t[:, 1:]], axis=-1)


def _attention_call(kernel, name, lat, ctx, consts, ctx_out, scratch=()):
    batch = lat[0].shape[0] // SEQ
    lat_blk = pl.BlockSpec((SEQ, GROUP_W), lambda b: (b, 0))
    ctx_blk = pl.BlockSpec((CTX_LEN, GROUP_W), lambda b: (b, 0))
    in_specs = [lat_blk] * 3 + [ctx_blk] * 2 + [_resident(c.shape[len(lead):], lead) for c, lead in consts]
    args = list(lat) + [ctx[1], ctx[2]] + [c for c, _ in consts]
    out_shape = [jax.ShapeDtypeStruct((batch * SEQ, GROUP_W), BF16)]
    out_specs = [lat_blk]
    if ctx_out:
        in_specs.append(ctx_blk)
        args.append(ctx[0])
        out_shape.append(jax.ShapeDtypeStruct((batch * CTX_LEN, GROUP_W), BF16))
        out_specs.append(ctx_blk)
    outs = pl.pallas_call(
        functools.partial(kernel, ctx_out=ctx_out), grid=(batch,), in_specs=in_specs,
        out_specs=out_specs, out_shape=out_shape, scratch_shapes=list(scratch),
        compiler_params=_params(1), name=name,
    )(*args)
    return outs if ctx_out else (outs[0], None)


SWA_HEADS_PER_CHAIN = 2


def _swa_kernel(*refs, ctx_out):
    n_io = 11 if ctx_out else 9
    scratch = refs[n_io:]
    if ctx_out:
        q_ref, k_ref, v_ref, kx_ref, vx_ref, sink_ref, csink_ref, cq_ref, o_ref, oc_ref, win_ref = refs[:n_io]
    else:
        q_ref, k_ref, v_ref, kx_ref, vx_ref, sink_ref, csink_ref, o_ref, win_ref = refs[:n_io]
    hpc = SWA_HEADS_PER_CHAIN
    chain_rows = hpc * SWA_BLOCK
    band = 3 * SWA_BLOCK
    n_blocks = SEQ // SWA_BLOCK

    @pl.when(pl.program_id(0) == 0)
    def _():
        shape = (chain_rows, band)
        rel = (lax.broadcasted_iota(jnp.int32, shape, 1)
               - lax.broadcasted_iota(jnp.int32, shape, 0) % SWA_BLOCK)
        for i, shift in enumerate((0, -SWA_BLOCK, -2 * SWA_BLOCK)):
            win_ref[i] = jnp.where(jnp.abs(rel + shift) <= SWA_WINDOW, 0.0, NEG)

    masks = [_head_mask(SWA_BLOCK, hpc, g * hpc) for g in range(N_HEADS // hpc)]
    masks_b = [mask.astype(BF16) for mask in masks]

    def geometry(nb):
        q0 = pl.multiple_of(nb * SWA_BLOCK, SWA_BLOCK)
        return q0, pl.multiple_of(jnp.clip(q0 - SWA_BLOCK, 0, SEQ - band), SWA_BLOCK)

    def scores(nb, slot):
        q0, k0 = geometry(nb)
        placement = jnp.where(nb == 0, 0, jnp.where(nb == n_blocks - 1, 2, 1))
        q = q_ref[pl.ds(q0, SWA_BLOCK), :]
        for g in range(len(masks)):
            qst = _stack_heads(q, masks_b[g])
            scratch[slot][g] = _dot_nt(qst, k_ref[pl.ds(k0, band), :]) + win_ref[placement]
            scratch[2 + slot][g] = _dot_nt(qst, kx_ref[...])

    def softmax(slot):
        for g in range(len(masks)):
            _softmax_stage(scratch, slot, g, sink_ref[g * chain_rows:(g + 1) * chain_rows, :])

    def pv(nb, slot):
        q0, k0 = geometry(nb)
        out = None
        for g, mask in enumerate(masks):
            o = _pv_stage(scratch, slot, g, v_ref[pl.ds(k0, band), :], vx_ref[...])
            og = _unstack_heads(o, mask, SWA_BLOCK)
            out = og if out is None else out + og
        o_ref[pl.ds(q0, SWA_BLOCK), :] = out.astype(BF16)

    _pipelined_loop(n_blocks, scores, softmax, pv)
    if ctx_out:
        _ctx_dense(cq_ref, kx_ref[...], vx_ref[...], csink_ref, oc_ref)


def _rope_tables():
    t = np.arange(SEQ)
    pos = np.stack([t // GRID_W, t % GRID_W], axis=1).astype(np.float32)
    half = HEAD_DIM // 4
    inv = ROPE_BASE ** (-jnp.arange(half, dtype=F32) / half)
    ang = jnp.asarray(pos)[:, :, None] * inv[None, None, :]
    cos = jnp.cos(ang)
    sin = jnp.sin(ang)
    cos64 = jnp.concatenate([cos, cos], axis=-1).reshape(SEQ, HEAD_DIM)
    sin64 = jnp.concatenate([-sin, sin], axis=-1).reshape(SEQ, HEAD_DIM)
    return jnp.tile(cos64, (1, V7X_LANES // HEAD_DIM)), jnp.tile(sin64, (1, V7X_LANES // HEAD_DIM))


def kernel(x, c, ctx, c_ctx, w_ada, b_ada, norm_g, ffn_w1, ffn_w3, ffn_w2, w_in, conv_w, conv_b, conv_ln_g,
           conv_ln_b, nat_rel_bias, sink_logits, w_out):
    bsz, seq, d = x.shape
    depth = w_ada.shape[0]
    assert (seq, d, ctx.shape[1]) == (SEQ, D_MODEL, CTX_LEN) and bsz < MOD_ROWS
    xl = x.reshape(bsz * seq, d)
    xc = ctx.reshape(bsz * CTX_LEN, d)

    cv = jnp.zeros((MOD_ROWS, d), F32).at[:bsz].set(c).at[bsz].set(c_ctx)
    mod = _ada_mod(cv, w_ada, b_ada).reshape(depth, MOD_ROWS, N_MOD, d)
    rope_tabs = _rope_tables()
    dft_lat = _dft_tables(SEQ)
    dft_ctx = _dft_tables(CTX_LEN)
    w1, w3, w2 = ffn_w1.astype(BF16), ffn_w3.astype(BF16), ffn_w2.astype(BF16)
    wx = w_in.astype(BF16)
    wo = w_out.astype(BF16)
    conv_p = jnp.stack([conv_b, conv_ln_g, conv_ln_b], axis=1)
    nat_bias = _nat_bias_table(nat_rel_bias)
    sink_lat = jnp.repeat(sink_logits * LOG2_E, SWA_BLOCK, axis=1)[..., None]
    sink_ctx = jnp.repeat(sink_logits * LOG2_E, CTX_LEN, axis=1)[..., None]
    swa_rows, swa_chains = SWA_HEADS_PER_CHAIN * SWA_BLOCK, N_HEADS // SWA_HEADS_PER_CHAIN
    swa_scratch = ((pltpu.VMEM((3, swa_rows, 3 * SWA_BLOCK), F32),)
                   + _pipeline_scratch(swa_chains, swa_rows, 3 * SWA_BLOCK))
    nat_scratch = _pipeline_scratch(NAT_ROWS_PER_STEP, N_HEADS * GRID_W, NAT_ROWS * GRID_W)

    for l in range(depth):
        last = l == depth - 1
        xl1, ua, ub, qn, kn, vn, qs, ks, vs = _front(xl, mod, 0, SEQ, l, norm_g, w1, w3, w2, wx, rope_tabs)
        xc1, ca, cb, cqn, ckn, cvn, cqs, cks, cvs = _front(xc, mod, bsz, bsz * CTX_LEN, l, norm_g, w1, w3, w2,
                                                           wx, None)
        y_a = _conv(ua, conv_w, conv_p, l, SEQ)
        y_b = _fourier(ub, *dft_lat, SEQ)
        y_c, yc_c = _attention_call(_nat_kernel, "nat", (qn, kn, vn), (cqn, ckn, cvn), ((nat_bias, (l,)),),
                                    not last, scratch=nat_scratch)
        y_d, yc_d = _attention_call(_swa_kernel, "swa", (qs, ks, vs), (cqs, cks, cvs),
                                    ((sink_lat, (l,)), (sink_ctx, (l,))), not last, scratch=swa_scratch)
        xl = _back(xl1, (y_a, y_b, y_c, y_d), mod, 0, SEQ, l, norm_g, wo, w1, w3, w2)
        if not last:
            yc_a = _conv(ca, conv_w, conv_p, l, CTX_LEN)
            yc_b = _fourier(cb, *dft_ctx, CTX_LEN)
            xc = _back(xc1, (yc_a, yc_b, yc_c, yc_d), mod, bsz, bsz * CTX_LEN, l, norm_g, wo, w1, w3, w2)
    return xl.reshape(bsz, seq, d)
```

```python
import functools

import numpy as np
import jax
import jax.numpy as jnp
from jax import lax
from jax.experimental import pallas as pl
from jax.experimental.pallas import tpu as pltpu

F32 = jnp.float32
BF16 = jnp.bfloat16

D_MODEL = 1024
SEQ = 2048
CTX_LEN = 256
GRID_W = 64
GRID_H = SEQ // GRID_W
GROUP_W = 256
HEAD_DIM = 64
N_HEADS = 4
CONV_K = 31
NAT_ROWS = 8
NAT_COLS = 16
SWA_KV_HEADS = 2
SWA_WINDOW = 128
SWA_BLOCK = 128
ROPE_BASE = 10000.0
FFN_DIM = 2816
MACARON_W = 0.5
N_MOD = 9
MOD_ROWS = 24
EPS = 1e-6
NEG = -1e30
LOG2_E = float(np.log2(np.e))
Q_SCALE = HEAD_DIM ** -0.5 * LOG2_E

V7X_LANES = 128
V7X_SUBLANES = 8
V7X_MXU_DIM = 256
V7X_VMEM_LIMIT = 56 * 1024 * 1024
ROW_TILE = 512
SUB_TILE_ROWS = (256, 256)
FFN_CHUNK_BOUNDS = (0, 6 * V7X_MXU_DIM, FFN_DIM)
CONV_HALO = 2 * V7X_SUBLANES

COL_UA = 0
COL_UB = 512
COL_QN, COL_KN, COL_VN = 768, 1024, 1280
COL_QS, COL_KS, COL_VS = 1536, 1792, 1920
IN_DIM = 2048


def _resident(shape, lead=()):
    nd = len(shape)
    return pl.BlockSpec((None,) * len(lead) + tuple(shape), lambda *_: tuple(lead) + (0,) * nd,
                        pipeline_mode=pl.Buffered(1))


def _params(n_axes=1):
    return pltpu.CompilerParams(dimension_semantics=("arbitrary",) * n_axes, vmem_limit_bytes=V7X_VMEM_LIMIT)


def _rms(x, g):
    return x * lax.rsqrt(jnp.mean(x * x, axis=-1, keepdims=True) + EPS) * g


def _sigmoid(x):
    return 1.0 / (1.0 + jnp.exp(-x))


def _sub_tiles(rows):
    assert rows == sum(SUB_TILE_ROWS)
    starts = np.cumsum((0,) + SUB_TILE_ROWS)
    return [slice(int(a), int(b)) for a, b in zip(starts[:-1], starts[1:])]


def _ffn(xs, shift, scale, gate, g_pre, g_post, w1_ref, w3_ref, w2_ref):
    hs = [(_rms(x, g_pre) * (1.0 + scale) + shift).astype(BF16) for x in xs]
    ys = [None] * len(xs)
    for lo, hi in zip(FFN_CHUNK_BOUNDS[:-1], FFN_CHUNK_BOUNDS[1:]):
        ts = []
        for h in hs:
            a = jnp.dot(h, w1_ref[:, lo:hi], preferred_element_type=F32)
            b = jnp.dot(h, w3_ref[:, lo:hi], preferred_element_type=F32)
            ts.append((a * _sigmoid(a) * b).astype(BF16))
        for i, t in enumerate(ts):
            yc = jnp.dot(t, w2_ref[lo:hi, :], preferred_element_type=F32)
            ys[i] = yc if ys[i] is None else ys[i] + yc
    return [x + (MACARON_W * gate) * _rms(y, g_post) for x, y in zip(xs, ys)]


def _cast_kernel(w_ref, o_ref):
    o_ref[...] = w_ref[...].astype(BF16)


def _to_bf16(w):
    cols = w.shape[-1]
    flat = w.reshape(-1, cols)
    br = 1024
    assert flat.shape[0] % br == 0
    blk = pl.BlockSpec((br, cols), lambda i: (i, 0))
    out = pl.pallas_call(
        _cast_kernel, grid=(flat.shape[0] // br,), in_specs=[blk], out_specs=blk,
        out_shape=jax.ShapeDtypeStruct(flat.shape, BF16), compiler_params=_params(1), name="to_bf16",
    )(flat)
    return out.reshape(w.shape)


def _ada_kernel(cv_ref, w_ref, b_ref, o_ref):
    c = cv_ref[...]
    sc = (c * _sigmoid(c)).astype(BF16)
    o_ref[...] = jnp.dot(sc, w_ref[...].astype(BF16), preferred_element_type=F32) + b_ref[...]


def _ada_mod(cv, w_ada, b_ada):
    depth, d, n = w_ada.shape
    rows = cv.shape[0]
    tn = 9 * V7X_LANES
    return pl.pallas_call(
        _ada_kernel,
        grid=(depth, n // tn),
        in_specs=[pl.BlockSpec((rows, d), lambda l, j: (0, 0)),
                  pl.BlockSpec((None, d, tn), lambda l, j: (l, 0, j)),
                  pl.BlockSpec((None, 1, tn), lambda l, j: (l, 0, j))],
        out_specs=pl.BlockSpec((None, rows, tn), lambda l, j: (l, 0, j)),
        out_shape=jax.ShapeDtypeStruct((depth, rows, n), F32),
        compiler_params=_params(2),
        name="ada_mod",
    )(cv, w_ada, b_ada.reshape(depth, 1, n))


def _lane_index(shape):
    return lax.broadcasted_iota(jnp.int32, shape, 1)


def _rotate_half_partner(t):
    quarter = HEAD_DIM // 4
    up = pltpu.roll(t, V7X_LANES - quarter, axis=1)
    down = pltpu.roll(t, quarter, axis=1)
    return jnp.where(_lane_index(t.shape) % (2 * quarter) < quarter, up, down)


def _per_query_head(t):
    swapped = pltpu.roll(t, HEAD_DIM, axis=1)
    first = _lane_index(t.shape) < HEAD_DIM
    return jnp.concatenate([jnp.where(first, t, swapped), jnp.where(first, swapped, t)], axis=1)


def _front_kernel(*refs, rope):
    if rope:
        (x_ref, mod_ref, g_ref, w1_ref, w3_ref, w2_ref, wx_ref, cos_ref, sin_ref,
         x1_ref, ua_ref, ub_ref, qn_ref, kn_ref, vn_ref, qs_ref, ks_ref, vs_ref) = refs
    else:
        (x_ref, mod_ref, g_ref, w1_ref, w3_ref, w2_ref, wx_ref,
         x1_ref, ua_ref, ub_ref, qn_ref, kn_ref, vn_ref, qs_ref, ks_ref, vs_ref) = refs
    tiles = _sub_tiles(x_ref.shape[0])
    x1s = _ffn([x_ref[rows, :] for rows in tiles], mod_ref[0:1, :], mod_ref[1:2, :], mod_ref[2:3, :],
               g_ref[0:1, :], g_ref[1:2, :], w1_ref, w3_ref, w2_ref)
    hs = []
    for rows, x1 in zip(tiles, x1s):
        x1_ref[rows, :] = x1
        hs.append((_rms(x1, g_ref[2:3, :]) * (1.0 + mod_ref[4:5, :]) + mod_ref[3:4, :]).astype(BF16))
    us = [jnp.dot(h, wx_ref[...], preferred_element_type=F32) for h in hs]
    for rows, u in zip(tiles, us):
        ua_ref[rows, :] = u[:, COL_UA:COL_UA + 512]
        ub_ref[rows, :] = u[:, COL_UB:COL_UB + 256].astype(BF16)
        qn_ref[rows, :] = (u[:, COL_QN:COL_QN + 256] * Q_SCALE).astype(BF16)
        kn_ref[rows, :] = u[:, COL_KN:COL_KN + 256].astype(BF16)
        vn_ref[rows, :] = u[:, COL_VN:COL_VN + 256].astype(BF16)
        q_tiles = [u[:, COL_QS + i * V7X_LANES:COL_QS + (i + 1) * V7X_LANES] for i in range(2)]
        ks = u[:, COL_KS:COL_KS + V7X_LANES]
        if rope:
            cos, sin = cos_ref[rows, :], sin_ref[rows, :]
            q_tiles = [t * cos + _rotate_half_partner(t) * sin for t in q_tiles]
            ks = ks * cos + _rotate_half_partner(ks) * sin
        qs_ref[rows, :] = (jnp.concatenate(q_tiles, axis=1) * Q_SCALE).astype(BF16)
        ks_ref[rows, :] = _per_query_head(ks).astype(BF16)
        vs_ref[rows, :] = _per_query_head(u[:, COL_VS:COL_VS + V7X_LANES]).astype(BF16)


def _mod_spec(layer, row0, tiles_per_batch):
    return pl.BlockSpec((None, None, N_MOD, D_MODEL), lambda i: (layer, row0 + i // tiles_per_batch, 0, 0))


def _front(x, mod, mod_row0, rows_per_batch, layer, g, w1, w3, w2, wx, rope_tabs):
    rows, d = x.shape
    tm = ROW_TILE
    tiles_per_batch = rows_per_batch // tm
    rope = rope_tabs is not None
    row_tile = lambda w: pl.BlockSpec((tm, w), lambda i: (i, 0))
    in_specs = [row_tile(d), _mod_spec(layer, mod_row0, tiles_per_batch),
                _resident(g.shape[1:], (layer,)), _resident(w1.shape[2:], (layer, 0)),
                _resident(w3.shape[2:], (layer, 0)), _resident(w2.shape[2:], (layer, 0)),
                _resident(wx.shape[1:], (layer,))]
    args = [x, mod, g, w1, w3, w2, wx]
    if rope:
        tab = pl.BlockSpec((tm, V7X_LANES), lambda i: (i % tiles_per_batch, 0))
        in_specs += [tab, tab]
        args += list(rope_tabs)
    out_shape = ([jax.ShapeDtypeStruct((rows, d), F32), jax.ShapeDtypeStruct((rows, 512), F32)]
                 + [jax.ShapeDtypeStruct((rows, 256), BF16)] * 7)
    out_specs = [row_tile(d), row_tile(512)] + [row_tile(256)] * 7
    return pl.pallas_call(
        functools.partial(_front_kernel, rope=rope),
        grid=(rows // tm,), in_specs=in_specs, out_specs=out_specs, out_shape=out_shape,
        compiler_params=_params(1), name="front_rope" if rope else "front_ctx",
    )(*args)


def _back_kernel(x_ref, ya_ref, yb_ref, yc_ref, yd_ref, mod_ref, g_ref, wo_ref, w1_ref, w3_ref, w2_ref, o_ref):
    tiles = _sub_tiles(x_ref.shape[0])
    x2s = []
    for rows in tiles:
        y = None
        for j, y_ref in enumerate((ya_ref, yb_ref, yc_ref, yd_ref)):
            yj = jnp.dot(y_ref[rows, :], wo_ref[j * GROUP_W:(j + 1) * GROUP_W, :], preferred_element_type=F32)
            y = yj if y is None else y + yj
        x2s.append(x_ref[rows, :] + mod_ref[5:6, :] * _rms(y, g_ref[3:4, :]))
    outs = _ffn(x2s, mod_ref[6:7, :], mod_ref[7:8, :], mod_ref[8:9, :],
                g_ref[4:5, :], g_ref[5:6, :], w1_ref, w3_ref, w2_ref)
    for rows, out in zip(tiles, outs):
        o_ref[rows, :] = out


def _back(x, ys, mod, mod_row0, rows_per_batch, layer, g, wo, w1, w3, w2):
    rows, d = x.shape
    tm = ROW_TILE
    tiles_per_batch = rows_per_batch // tm
    row_tile = lambda w: pl.BlockSpec((tm, w), lambda i: (i, 0))
    in_specs = ([row_tile(d)] + [row_tile(GROUP_W)] * 4
                + [_mod_spec(layer, mod_row0, tiles_per_batch),
                   _resident(g.shape[1:], (layer,)), _resident(wo.shape[1:], (layer,)),
                   _resident(w1.shape[2:], (layer, 1)), _resident(w3.shape[2:], (layer, 1)),
                   _resident(w2.shape[2:], (layer, 1))])
    return pl.pallas_call(
        _back_kernel, grid=(rows // tm,), in_specs=in_specs,
        out_specs=row_tile(d), out_shape=jax.ShapeDtypeStruct((rows, d), F32),
        compiler_params=_params(1), name="back",
    )(x, *ys, mod, g, wo, w1, w3, w2)


def _conv_kernel(prev_ref, cur_ref, next_ref, w_ref, p_ref, o_ref, pad_ref, *, n_chunks, rc, sub):
    c = pl.program_id(1)

    def glu(u):
        return u[:, :GROUP_W] * _sigmoid(u[:, GROUP_W:])

    pad_ref[0:CONV_HALO, :] = jnp.where(c > 0, glu(prev_ref[...]), 0.0)
    pad_ref[CONV_HALO:CONV_HALO + rc, :] = glu(cur_ref[...])
    pad_ref[CONV_HALO + rc:, :] = jnp.where(c < n_chunks - 1, glu(next_ref[...]), 0.0)
    bias, ln_g, ln_b = p_ref[0:1, :], p_ref[1:2, :], p_ref[2:3, :]
    lead = CONV_HALO - CONV_K // 2
    zrows = sub + V7X_SUBLANES
    for s in range(rc // sub):
        acc = jnp.zeros((sub, GROUP_W), F32) + bias
        for b in range(V7X_SUBLANES):
            z = None
            for a in range((lead + CONV_K - 1) // V7X_SUBLANES + 1):
                k = V7X_SUBLANES * a + b - lead
                if 0 <= k < CONV_K:
                    r0 = s * sub + V7X_SUBLANES * a
                    term = pad_ref[r0:r0 + zrows, :] * w_ref[k:k + 1, :]
                    z = term if z is None else z + term
            acc = acc + z[b:b + sub, :]
        mu = jnp.mean(acc, axis=-1, keepdims=True)
        cen = acc - mu
        var = jnp.mean(cen * cen, axis=-1, keepdims=True)
        yn = cen * lax.rsqrt(var + EPS) * ln_g + ln_b
        o_ref[s * sub:(s + 1) * sub, :] = (yn * _sigmoid(yn)).astype(BF16)


def _conv(ua, conv_w, conv_p, layer, rows_per_batch):
    rows = ua.shape[0]
    batch = rows // rows_per_batch
    rc = 256
    n_chunks = rows_per_batch // rc
    hb = rc // CONV_HALO
    n_halo = rows // CONV_HALO
    cur = pl.BlockSpec((rc, 512), lambda b, c: (b * n_chunks + c, 0))
    prev = pl.BlockSpec((CONV_HALO, 512), lambda b, c: (jnp.maximum((b * n_chunks + c) * hb - 1, 0), 0))
    nxt = pl.BlockSpec((CONV_HALO, 512), lambda b, c: (jnp.minimum((b * n_chunks + c + 1) * hb, n_halo - 1), 0))
    return pl.pallas_call(
        functools.partial(_conv_kernel, n_chunks=n_chunks, rc=rc, sub=64),
        grid=(batch, n_chunks),
        in_specs=[prev, cur, nxt, _resident(conv_w.shape[1:], (layer,)), _resident(conv_p.shape[1:], (layer,))],
        out_specs=pl.BlockSpec((rc, GROUP_W), lambda b, c: (b * n_chunks + c, 0)),
        out_shape=jax.ShapeDtypeStruct((rows, GROUP_W), BF16),
        scratch_shapes=[pltpu.VMEM((rc + 2 * CONV_HALO, GROUP_W), F32)],
        compiler_params=_params(2), name="conv",
    )(ua, ua, ua, conv_w, conv_p)


FOURIER_BATCHES_PER_STEP = 2


def _fourier_kernel(z_ref, wc_ref, m_ref, o_ref, *, n, scale):
    seqs = [slice(i * n, (i + 1) * n) for i in range(FOURIER_BATCHES_PER_STEP)]
    ts = [jnp.dot(z_ref[rows, :], wc_ref[...], preferred_element_type=F32) for rows in seqs]
    zzs = [jnp.concatenate([t[:, :GROUP_W], t[:, GROUP_W:]], axis=0).astype(BF16) for t in ts]
    ys = [jnp.dot(m_ref[...], zz, preferred_element_type=F32) for zz in zzs]
    for rows, y in zip(seqs, ys):
        o_ref[rows, :] = (y * scale).astype(BF16)


def _unit_circle(n, period):
    i = jnp.arange(n, dtype=jnp.int32)[:, None]
    j = jnp.arange(period, dtype=jnp.int32)[None, :]
    ang = (2.0 * np.pi / period) * ((i * j) % period).astype(F32)
    return jnp.cos(ang), jnp.sin(ang)


def _dft_tables(n):
    hi = max(n // GRID_W, 1)
    lo = n // hi
    k = jnp.arange(n, dtype=jnp.int32)[None, :]
    n1 = jnp.arange(hi, dtype=jnp.int32)[:, None]
    n0 = jnp.arange(lo, dtype=jnp.int32)[:, None]
    a1 = (2.0 * np.pi / hi) * ((n1 * k) % hi).astype(F32)
    a0 = (2.0 * np.pi / n) * ((n0 * k) % n).astype(F32)
    c1, s1, c0, s0 = jnp.cos(a1)[:, None], jnp.sin(a1)[:, None], jnp.cos(a0)[None], jnp.sin(a0)[None]
    cos = (c1 * c0 - s1 * s0).reshape(n, n)
    sin = (s1 * c0 + c1 * s0).reshape(n, n)
    m = jnp.concatenate([cos, -sin], axis=1).astype(BF16)
    cc, sc = _unit_circle(HEAD_DIM, HEAD_DIM)
    eye = jnp.eye(GROUP_W // HEAD_DIM, dtype=F32)
    wc = jnp.concatenate([jnp.kron(eye, cc), jnp.kron(eye, sc)], axis=1).astype(BF16)
    return wc, m


def _fourier(ub, wc, m, rows_per_batch):
    rows = ub.shape[0]
    n = rows_per_batch
    step_rows = FOURIER_BATCHES_PER_STEP * n
    blk = pl.BlockSpec((step_rows, GROUP_W), lambda b: (b, 0))
    return pl.pallas_call(
        functools.partial(_fourier_kernel, n=n, scale=float((n * HEAD_DIM) ** -0.5)),
        grid=(rows // step_rows,), in_specs=[blk, _resident(wc.shape), _resident(m.shape)], out_specs=blk,
        out_shape=jax.ShapeDtypeStruct((rows, GROUP_W), BF16),
        compiler_params=_params(1), name="fourier",
    )(ub, wc, m)


def _head_mask(rows_per_head, heads=N_HEADS, head0=0):
    shape = (heads * rows_per_head, GROUP_W)
    row_h = lax.broadcasted_iota(jnp.int32, shape, 0) // rows_per_head + head0
    lane_h = lax.broadcasted_iota(jnp.int32, shape, 1) // HEAD_DIM
    return (row_h == lane_h).astype(F32)


def _stack_heads(q, mask_bf16):
    heads = mask_bf16.shape[0] // q.shape[0]
    return jnp.concatenate([q] * heads, axis=0) * mask_bf16


def _unstack_heads(o, rows_per_head, head0=0):
    heads = o.shape[0] // rows_per_head
    assert heads % 2 == 0 and head0 % 2 == 0
    first = _lane_index((rows_per_head, V7X_LANES)) < HEAD_DIM
    tiles = []
    for pair in range(heads // 2):
        lanes = slice((head0 // 2 + pair) * V7X_LANES, (head0 // 2 + pair + 1) * V7X_LANES)
        lo = o[2 * pair * rows_per_head:(2 * pair + 1) * rows_per_head, lanes]
        hi = o[(2 * pair + 1) * rows_per_head:(2 * pair + 2) * rows_per_head, lanes]
        tiles.append(jnp.where(first, lo, hi))
    return tiles[0] if len(tiles) == 1 else jnp.concatenate(tiles, axis=1)


def _dot_nt(a, b):
    return lax.dot_general(a, b, (((1,), (1,)), ((), ())), preferred_element_type=F32)


def _tree(op, xs):
    while len(xs) > 1:
        xs = [op(xs[i], xs[i + 1]) if i + 1 < len(xs) else xs[i] for i in range(0, len(xs), 2)]
    return xs[0]


def _lane_tiles(*arrays):
    return [a[:, i:i + V7X_LANES] for a in arrays for i in range(0, a.shape[1], V7X_LANES)]


def _ctx_dense(cq_ref, kx, vx, sink_ref, oc_ref):
    hpc = 2
    outs = []
    for g in range(N_HEADS // hpc):
        qst = _stack_heads(cq_ref[...], _head_mask(CTX_LEN, hpc, g * hpc).astype(BF16))
        s = _dot_nt(qst, kx)
        m = jnp.max(_tree(jnp.maximum, _lane_tiles(s)), axis=-1, keepdims=True)
        sink = None if sink_ref is None else sink_ref[g * hpc * CTX_LEN:(g + 1) * hpc * CTX_LEN, :]
        if sink is not None:
            m = jnp.maximum(m, sink)
        p = jnp.exp2(s - m)
        den = jnp.sum(_tree(jnp.add, _lane_tiles(p)), axis=-1, keepdims=True)
        if sink is not None:
            den = den + jnp.exp2(sink - m)
        o = jnp.dot(p.astype(BF16), vx, preferred_element_type=F32) / den
        outs.append(_unstack_heads(o, CTX_LEN, g * hpc))
    oc_ref[...] = jnp.concatenate(outs, axis=1).astype(BF16)


NAT_ROWS_PER_STEP = 2


def _pipelined_loop(n_steps, scores_fn, softmax_fn, pv_fn):
    scores_fn(0, 0)

    def pair(it, carry):
        for slot in (0, 1):
            step = 2 * it + slot
            scores_fn(jnp.minimum(step + 1, n_steps - 1), 1 - slot)
            softmax_fn(slot)
            pv_fn(step, slot)
        return carry

    lax.fori_loop(0, n_steps // 2, pair, 0)


def _pipeline_scratch(chains, rows, n_loc):
    kinds = [((chains, rows, n_loc), F32), ((chains, rows, CTX_LEN), F32),
             ((chains, rows, n_loc), BF16), ((chains, rows, CTX_LEN), BF16), ((chains, rows, 1), F32)]
    return tuple(pltpu.VMEM(shape, dtype) for shape, dtype in kinds for _ in range(2))


def _softmax_stage(scratch, slot, chain, sink):
    sl, sc, pl_, pc, dn = (scratch[2 * i + slot] for i in range(5))
    s, c = sl[chain], sc[chain]
    m = jnp.max(_tree(jnp.maximum, _lane_tiles(s, c)), axis=-1, keepdims=True)
    if sink is not None:
        m = jnp.maximum(m, sink)
    p = jnp.exp2(s - m)
    q = jnp.exp2(c - m)
    den = jnp.sum(_tree(jnp.add, _lane_tiles(p, q)), axis=-1, keepdims=True)
    if sink is not None:
        den = den + jnp.exp2(sink - m)
    pl_[chain] = p.astype(BF16)
    pc[chain] = q.astype(BF16)
    dn[chain] = den


def _pv_stage(scratch, slot, chain, vb, vx):
    pl_, pc, dn = (scratch[2 * i + slot] for i in (2, 3, 4))
    o = (jnp.dot(pl_[chain], vb, preferred_element_type=F32) + jnp.dot(pc[chain], vx, preferred_element_type=F32))
    return o / dn[chain]


def _nat_stages(q_ref, k_ref, v_ref, kx_ref, vx_ref, bias_ref, o_ref, scratch):
    mask_b = _head_mask(GRID_W).astype(BF16)
    band = NAT_ROWS * GRID_W
    rps = NAT_ROWS_PER_STEP

    def geometry(r):
        rs = jnp.clip(r - NAT_ROWS // 2, 0, GRID_H - NAT_ROWS)
        return pl.multiple_of(r * GRID_W, GRID_W), pl.multiple_of(rs * GRID_W, GRID_W), rs - r + NAT_ROWS - 1

    def scores(step, slot):
        for j in range(rps):
            q0, k0, d0 = geometry(step * rps + j)
            qst = _stack_heads(q_ref[pl.ds(q0, GRID_W), :], mask_b)
            bias = jnp.concatenate([bias_ref[d0 + 2 * a] for a in range(NAT_ROWS // 2)], axis=1)
            scratch[slot][j] = _dot_nt(qst, k_ref[pl.ds(k0, band), :]) + bias
            scratch[2 + slot][j] = _dot_nt(qst, kx_ref[...])

    def softmax(slot):
        for j in range(rps):
            _softmax_stage(scratch, slot, j, None)

    def pv(step, slot):
        for j in range(rps):
            q0, k0, _ = geometry(step * rps + j)
            o = _pv_stage(scratch, slot, j, v_ref[pl.ds(k0, band), :], vx_ref[...])
            o_ref[pl.ds(q0, GRID_W), :] = _unstack_heads(o, GRID_W).astype(BF16)

    return scores, softmax, pv


def _nat_bias_table(rel_bias):
    depth = rel_bias.shape[0]
    w = GRID_W
    n_dr = 2 * NAT_ROWS - 1
    lo = w - NAT_COLS
    g = jnp.pad(rel_bias.astype(F32) * LOG2_E, ((0, 0),) * 3 + ((lo, 2 * w - lo - (2 * NAT_COLS - 1)),))
    flat = jnp.tile(g, (1, 1, 1, w))[..., :w * (2 * w - 1)]
    t = flat.reshape(depth, N_HEADS, n_dr, w, 2 * w - 1)[..., w - 1:]
    qc = np.arange(w)[:, None]
    kc = np.arange(w)[None, :]
    ws = np.clip(qc - NAT_COLS // 2, 0, w - NAT_COLS)
    col_ok = (kc >= ws) & (kc < ws + NAT_COLS)
    t = jnp.where(col_ok, t, NEG).transpose(0, 2, 1, 3, 4).reshape(depth, n_dr, N_HEADS * w, w)
    return jnp.concatenate([t[:, :-1], t[:, 1:]], axis=-1)


SWA_HEADS_PER_CHAIN = 2


def _swa_stages(q_ref, k_ref, v_ref, kx_ref, vx_ref, sink_ref, win_ref, o_ref, scratch):
    hpc = SWA_HEADS_PER_CHAIN
    chain_rows = hpc * SWA_BLOCK
    band = 3 * SWA_BLOCK
    n_blocks = SEQ // SWA_BLOCK

    @pl.when(pl.program_id(0) == 0)
    def _():
        shape = (chain_rows, band)
        rel = (lax.broadcasted_iota(jnp.int32, shape, 1)
               - lax.broadcasted_iota(jnp.int32, shape, 0) % SWA_BLOCK)
        for i, shift in enumerate((0, -SWA_BLOCK, -2 * SWA_BLOCK)):
            win_ref[i] = jnp.where(jnp.abs(rel + shift) <= SWA_WINDOW, 0.0, NEG)

    masks_b = [_head_mask(SWA_BLOCK, hpc, g * hpc).astype(BF16) for g in range(N_HEADS // hpc)]

    def geometry(nb):
        q0 = pl.multiple_of(nb * SWA_BLOCK, SWA_BLOCK)
        return q0, pl.multiple_of(jnp.clip(q0 - SWA_BLOCK, 0, SEQ - band), SWA_BLOCK)

    def scores(nb, slot):
        q0, k0 = geometry(nb)
        placement = jnp.where(nb == 0, 0, jnp.where(nb == n_blocks - 1, 2, 1))
        q = q_ref[pl.ds(q0, SWA_BLOCK), :]
        for g in range(len(masks_b)):
            qst = _stack_heads(q, masks_b[g])
            scratch[slot][g] = _dot_nt(qst, k_ref[pl.ds(k0, band), :]) + win_ref[placement]
            scratch[2 + slot][g] = _dot_nt(qst, kx_ref[...])

    def softmax(slot):
        for g in range(len(masks_b)):
            _softmax_stage(scratch, slot, g, sink_ref[g * chain_rows:(g + 1) * chain_rows, :])

    def pv(nb, slot):
        q0, k0 = geometry(nb)
        outs = []
        for g in range(len(masks_b)):
            o = _pv_stage(scratch, slot, g, v_ref[pl.ds(k0, band), :], vx_ref[...])
            outs.append(_unstack_heads(o, SWA_BLOCK, g * hpc))
        o_ref[pl.ds(q0, SWA_BLOCK), :] = jnp.concatenate(outs, axis=1).astype(BF16)

    return scores, softmax, pv


ATTENTION_STEPS = GRID_H // NAT_ROWS_PER_STEP
assert ATTENTION_STEPS == SEQ // SWA_BLOCK


def _attention_kernel(*refs, ctx_out):
    refs = list(refs)
    take = lambda n: [refs.pop(0) for _ in range(n)]
    nq, nk, nv, nkx, nvx, sq, sk, sv, skx, svx, bias_ref, sink_ref, csink_ref = take(13)
    cqn, cqs = take(2) if ctx_out else (None, None)
    o_nat, o_swa = take(2)
    oc_nat, oc_swa = take(2) if ctx_out else (None, None)
    (win_ref,), nat_scratch, swa_scratch = take(1), take(10), take(10)
    mixers = [_nat_stages(nq, nk, nv, nkx, nvx, bias_ref, o_nat, nat_scratch),
              _swa_stages(sq, sk, sv, skx, svx, sink_ref, win_ref, o_swa, swa_scratch)]

    def stage(i):
        return lambda *args: [mixer[i](*args) for mixer in mixers]

    _pipelined_loop(ATTENTION_STEPS, stage(0), stage(1), stage(2))
    if ctx_out:
        _ctx_dense(cqn, nkx[...], nvx[...], None, oc_nat)
        _ctx_dense(cqs, skx[...], svx[...], csink_ref, oc_swa)


def _attention(nat_lat, nat_ctx, swa_lat, swa_ctx, nat_bias, sink_lat, sink_ctx, layer, ctx_out):
    batch = nat_lat[0].shape[0] // SEQ
    lat_blk = pl.BlockSpec((SEQ, GROUP_W), lambda b: (b, 0))
    ctx_blk = pl.BlockSpec((CTX_LEN, GROUP_W), lambda b: (b, 0))
    consts = (nat_bias, sink_lat, sink_ctx)
    in_specs = ([lat_blk] * 3 + [ctx_blk] * 2) * 2 + [_resident(c.shape[1:], (layer,)) for c in consts]
    args = list(nat_lat) + list(nat_ctx[1:]) + list(swa_lat) + list(swa_ctx[1:]) + list(consts)
    out_shape = [jax.ShapeDtypeStruct((batch * SEQ, GROUP_W), BF16)] * 2
    out_specs = [lat_blk] * 2
    if ctx_out:
        in_specs += [ctx_blk] * 2
        args += [nat_ctx[0], swa_ctx[0]]
        out_shape += [jax.ShapeDtypeStruct((batch * CTX_LEN, GROUP_W), BF16)] * 2
        out_specs += [ctx_blk] * 2
    swa_rows, swa_chains = SWA_HEADS_PER_CHAIN * SWA_BLOCK, N_HEADS // SWA_HEADS_PER_CHAIN
    scratch = ((pltpu.VMEM((3, swa_rows, 3 * SWA_BLOCK), F32),)
               + _pipeline_scratch(NAT_ROWS_PER_STEP, N_HEADS * GRID_W, NAT_ROWS * GRID_W)
               + _pipeline_scratch(swa_chains, swa_rows, 3 * SWA_BLOCK))
    outs = pl.pallas_call(
        functools.partial(_attention_kernel, ctx_out=ctx_out), grid=(batch,), in_specs=in_specs,
        out_specs=out_specs, out_shape=out_shape, scratch_shapes=list(scratch),
        compiler_params=_params(1), name="attention",
    )(*args)
    return tuple(outs) if ctx_out else (outs[0], outs[1], None, None)


def _rope_tables():
    t = np.arange(SEQ)
    pos = np.stack([t // GRID_W, t % GRID_W], axis=1).astype(np.float32)
    half = HEAD_DIM // 4
    inv = ROPE_BASE ** (-jnp.arange(half, dtype=F32) / half)
    ang = jnp.asarray(pos)[:, :, None] * inv[None, None, :]
    cos = jnp.cos(ang)
    sin = jnp.sin(ang)
    cos64 = jnp.concatenate([cos, cos], axis=-1).reshape(SEQ, HEAD_DIM)
    sin64 = jnp.concatenate([-sin, sin], axis=-1).reshape(SEQ, HEAD_DIM)
    return jnp.tile(cos64, (1, V7X_LANES // HEAD_DIM)), jnp.tile(sin64, (1, V7X_LANES // HEAD_DIM))


def kernel(x, c, ctx, c_ctx, w_ada, b_ada, norm_g, ffn_w1, ffn_w3, ffn_w2, w_in, conv_w, conv_b, conv_ln_g,
           conv_ln_b, nat_rel_bias, sink_logits, w_out):
    bsz, seq, d = x.shape
    depth = w_ada.shape[0]
    assert (seq, d, ctx.shape[1]) == (SEQ, D_MODEL, CTX_LEN) and bsz < MOD_ROWS
    xl = x.reshape(bsz * seq, d)
    xc = ctx.reshape(bsz * CTX_LEN, d)

    cv = jnp.zeros((MOD_ROWS, d), F32).at[:bsz].set(c).at[bsz].set(c_ctx)
    mod = _ada_mod(cv, w_ada, b_ada).reshape(depth, MOD_ROWS, N_MOD, d)
    rope_tabs = _rope_tables()
    dft_lat = _dft_tables(SEQ)
    dft_ctx = _dft_tables(CTX_LEN)
    w1, w3, w2, wx, wo = (_to_bf16(w) for w in (ffn_w1, ffn_w3, ffn_w2, w_in, w_out))
    conv_p = jnp.stack([conv_b, conv_ln_g, conv_ln_b], axis=1)
    nat_bias = _nat_bias_table(nat_rel_bias)
    sink_lat = jnp.repeat(sink_logits * LOG2_E, SWA_BLOCK, axis=1)[..., None]
    sink_ctx = jnp.repeat(sink_logits * LOG2_E, CTX_LEN, axis=1)[..., None]

    for l in range(depth):
        last = l == depth - 1
        xl1, ua, ub, qn, kn, vn, qs, ks, vs = _front(xl, mod, 0, SEQ, l, norm_g, w1, w3, w2, wx, rope_tabs)
        xc1, ca, cb, cqn, ckn, cvn, cqs, cks, cvs = _front(xc, mod, bsz, bsz * CTX_LEN, l, norm_g, w1, w3, w2,
                                                           wx, None)
        y_a = _conv(ua, conv_w, conv_p, l, SEQ)
        y_b = _fourier(ub, *dft_lat, SEQ)
        y_c, y_d, yc_c, yc_d = _attention((qn, kn, vn), (cqn, ckn, cvn), (qs, ks, vs), (cqs, cks, cvs),
                                          nat_bias, sink_lat, sink_ctx, l, not last)
        xl = _back(xl1, (y_a, y_b, y_c, y_d), mod, 0, SEQ, l, norm_g, wo, w1, w3, w2)
        if not last:
            yc_a = _conv(ca, conv_w, conv_p, l, CTX_LEN)
            yc_b = _fourier(cb, *dft_ctx, CTX_LEN)
            xc = _back(xc1, (yc_a, yc_b, yc_c, yc_d), mod, bsz, bsz * CTX_LEN, l, norm_g, wo, w1, w3, w2)
    return xl.reshape(bsz, seq, d)
```

```python
import functools

import numpy as np
import jax
import jax.numpy as jnp
from jax import lax
from jax.experimental import pallas as pl
from jax.experimental.pallas import tpu as pltpu

F32 = jnp.float32
BF16 = jnp.bfloat16

D_MODEL = 1024
SEQ = 2048
CTX_LEN = 256
GRID_W = 64
GRID_H = SEQ // GRID_W
GROUP_W = 256
HEAD_DIM = 64
N_HEADS = 4
CONV_K = 31
NAT_ROWS = 8
NAT_COLS = 16
SWA_KV_HEADS = 2
SWA_WINDOW = 128
SWA_BLOCK = 128
ROPE_BASE = 10000.0
FFN_DIM = 2816
MACARON_W = 0.5
N_MOD = 9
MOD_ROWS = 24
EPS = 1e-6
NEG = -1e30
LOG2_E = float(np.log2(np.e))
Q_SCALE = HEAD_DIM ** -0.5 * LOG2_E

V7X_LANES = 128
V7X_SUBLANES = 8
V7X_MXU_DIM = 256
V7X_VMEM_LIMIT = 56 * 1024 * 1024
FRONT_ROW_TILE = 512
BACK_ROW_TILE = 1024
SUB_TILE_ROWS = 256
FFN_CHUNK_BOUNDS = (0, 6 * V7X_MXU_DIM, FFN_DIM)
CONV_HALO = 2 * V7X_SUBLANES

COL_UA = 0
COL_UB = 512
COL_QN, COL_KN, COL_VN = 768, 1024, 1280
COL_QS, COL_KS, COL_VS = 1536, 1792, 1920
IN_DIM = 2048


def _resident(shape, lead=()):
    nd = len(shape)
    return pl.BlockSpec((None,) * len(lead) + tuple(shape), lambda *_: tuple(lead) + (0,) * nd,
                        pipeline_mode=pl.Buffered(1))


def _params(n_axes=1):
    return pltpu.CompilerParams(dimension_semantics=("arbitrary",) * n_axes, vmem_limit_bytes=V7X_VMEM_LIMIT)


def _rms(x, g):
    return x * lax.rsqrt(jnp.mean(x * x, axis=-1, keepdims=True) + EPS) * g


def _sigmoid(x):
    return 1.0 / (1.0 + jnp.exp(-x))


def _sub_tiles(rows):
    assert rows % SUB_TILE_ROWS == 0
    return [slice(r0, r0 + SUB_TILE_ROWS) for r0 in range(0, rows, SUB_TILE_ROWS)]


def _ffn(xs, shift, scale, gate, g_pre, g_post, w1_ref, w3_ref, w2_ref):
    hs = [(_rms(x, g_pre) * (1.0 + scale) + shift).astype(BF16) for x in xs]
    ys = [None] * len(xs)
    for lo, hi in zip(FFN_CHUNK_BOUNDS[:-1], FFN_CHUNK_BOUNDS[1:]):
        ts = []
        for h in hs:
            a = jnp.dot(h, w1_ref[:, lo:hi], preferred_element_type=F32)
            b = jnp.dot(h, w3_ref[:, lo:hi], preferred_element_type=F32)
            ts.append((a * _sigmoid(a) * b).astype(BF16))
        for i, t in enumerate(ts):
            yc = jnp.dot(t, w2_ref[lo:hi, :], preferred_element_type=F32)
            ys[i] = yc if ys[i] is None else ys[i] + yc
    return [x + (MACARON_W * gate) * _rms(y, g_post) for x, y in zip(xs, ys)]


def _cast_kernel(w_ref, o_ref):
    o_ref[...] = w_ref[...].astype(BF16)


def _to_bf16(w):
    cols = w.shape[-1]
    flat = w.reshape(-1, cols)
    br = 256
    assert flat.shape[0] % br == 0
    blk = pl.BlockSpec((br, cols), lambda i: (i, 0))
    out = pl.pallas_call(
        _cast_kernel, grid=(flat.shape[0] // br,), in_specs=[blk], out_specs=blk,
        out_shape=jax.ShapeDtypeStruct(flat.shape, BF16), compiler_params=_params(1), name="to_bf16",
    )(flat)
    return out.reshape(w.shape)


def _ada_kernel(cv_ref, w_ref, b_ref, o_ref):
    c = cv_ref[...]
    sc = (c * _sigmoid(c)).astype(BF16)
    o_ref[...] = jnp.dot(sc, w_ref[...].astype(BF16), preferred_element_type=F32) + b_ref[...]


def _ada_mod(cv, w_ada, b_ada):
    depth, d, n = w_ada.shape
    rows = cv.shape[0]
    tn = 9 * V7X_LANES
    return pl.pallas_call(
        _ada_kernel,
        grid=(depth, n // tn),
        in_specs=[pl.BlockSpec((rows, d), lambda l, j: (0, 0)),
                  pl.BlockSpec((None, d, tn), lambda l, j: (l, 0, j)),
                  pl.BlockSpec((None, 1, tn), lambda l, j: (l, 0, j))],
        out_specs=pl.BlockSpec((None, rows, tn), lambda l, j: (l, 0, j)),
        out_shape=jax.ShapeDtypeStruct((depth, rows, n), F32),
        compiler_params=_params(2),
        name="ada_mod",
    )(cv, w_ada, b_ada.reshape(depth, 1, n))


def _lane_index(shape):
    return lax.broadcasted_iota(jnp.int32, shape, 1)


def _rotate_half_partner(t):
    quarter = HEAD_DIM // 4
    up = pltpu.roll(t, V7X_LANES - quarter, axis=1)
    down = pltpu.roll(t, quarter, axis=1)
    return jnp.where(_lane_index(t.shape) % (2 * quarter) < quarter, up, down)


def _per_query_head(t):
    swapped = pltpu.roll(t, HEAD_DIM, axis=1)
    first = _lane_index(t.shape) < HEAD_DIM
    return jnp.concatenate([jnp.where(first, t, swapped), jnp.where(first, swapped, t)], axis=1)


def _front_kernel(*refs, rope):
    if rope:
        (x_ref, mod_ref, g_ref, w1_ref, w3_ref, w2_ref, wx_ref, cos_ref, sin_ref,
         x1_ref, ua_ref, ub_ref, qn_ref, kn_ref, vn_ref, qs_ref, ks_ref, vs_ref) = refs
    else:
        (x_ref, mod_ref, g_ref, w1_ref, w3_ref, w2_ref, wx_ref,
         x1_ref, ua_ref, ub_ref, qn_ref, kn_ref, vn_ref, qs_ref, ks_ref, vs_ref) = refs
    tiles = _sub_tiles(x_ref.shape[0])
    x1s = _ffn([x_ref[rows, :] for rows in tiles], mod_ref[0:1, :], mod_ref[1:2, :], mod_ref[2:3, :],
               g_ref[0:1, :], g_ref[1:2, :], w1_ref, w3_ref, w2_ref)
    hs = []
    for rows, x1 in zip(tiles, x1s):
        x1_ref[rows, :] = x1
        hs.append((_rms(x1, g_ref[2:3, :]) * (1.0 + mod_ref[4:5, :]) + mod_ref[3:4, :]).astype(BF16))
    us = [jnp.dot(h, wx_ref[...], preferred_element_type=F32) for h in hs]
    for rows, u in zip(tiles, us):
        ua_ref[rows, :] = u[:, COL_UA:COL_UA + 512]
        ub_ref[rows, :] = u[:, COL_UB:COL_UB + 256].astype(BF16)
        qn_ref[rows, :] = (u[:, COL_QN:COL_QN + 256] * Q_SCALE).astype(BF16)
        kn_ref[rows, :] = u[:, COL_KN:COL_KN + 256].astype(BF16)
        vn_ref[rows, :] = u[:, COL_VN:COL_VN + 256].astype(BF16)
        q_tiles = [u[:, COL_QS + i * V7X_LANES:COL_QS + (i + 1) * V7X_LANES] for i in range(2)]
        ks = u[:, COL_KS:COL_KS + V7X_LANES]
        if rope:
            cos, sin = cos_ref[rows, :], sin_ref[rows, :]
            q_tiles = [t * cos + _rotate_half_partner(t) * sin for t in q_tiles]
            ks = ks * cos + _rotate_half_partner(ks) * sin
        qs_ref[rows, :] = (jnp.concatenate(q_tiles, axis=1) * Q_SCALE).astype(BF16)
        ks_ref[rows, :] = _per_query_head(ks).astype(BF16)
        vs_ref[rows, :] = _per_query_head(u[:, COL_VS:COL_VS + V7X_LANES]).astype(BF16)


def _mod_spec(layer, row0, tiles_per_batch):
    return pl.BlockSpec((None, None, N_MOD, D_MODEL), lambda i: (layer, row0 + i // tiles_per_batch, 0, 0))


def _front(x, mod, mod_row0, rows_per_batch, layer, g, w1, w3, w2, wx, rope_tabs):
    rows, d = x.shape
    tm = FRONT_ROW_TILE
    tiles_per_batch = rows_per_batch // tm
    rope = rope_tabs is not None
    row_tile = lambda w: pl.BlockSpec((tm, w), lambda i: (i, 0))
    in_specs = [row_tile(d), _mod_spec(layer, mod_row0, tiles_per_batch),
                _resident(g.shape[1:], (layer,)), _resident(w1.shape[2:], (layer, 0)),
                _resident(w3.shape[2:], (layer, 0)), _resident(w2.shape[2:], (layer, 0)),
                _resident(wx.shape[1:], (layer,))]
    args = [x, mod, g, w1, w3, w2, wx]
    if rope:
        tab = pl.BlockSpec((tm, V7X_LANES), lambda i: (i % tiles_per_batch, 0))
        in_specs += [tab, tab]
        args += list(rope_tabs)
    out_shape = ([jax.ShapeDtypeStruct((rows, d), F32), jax.ShapeDtypeStruct((rows, 512), F32)]
                 + [jax.ShapeDtypeStruct((rows, 256), BF16)] * 7)
    out_specs = [row_tile(d), row_tile(512)] + [row_tile(256)] * 7
    return pl.pallas_call(
        functools.partial(_front_kernel, rope=rope),
        grid=(rows // tm,), in_specs=in_specs, out_specs=out_specs, out_shape=out_shape,
        compiler_params=_params(1), name="front_rope" if rope else "front_ctx",
    )(*args)


def _back_kernel(x_ref, ya_ref, yb_ref, yc_ref, yd_ref, mod_ref, g_ref, wo_ref, w1_ref, w3_ref, w2_ref, o_ref):
    tiles = _sub_tiles(x_ref.shape[0])
    x2s = []
    for rows in tiles:
        y = None
        for j, y_ref in enumerate((ya_ref, yb_ref, yc_ref, yd_ref)):
            yj = jnp.dot(y_ref[rows, :], wo_ref[j * GROUP_W:(j + 1) * GROUP_W, :], preferred_element_type=F32)
            y = yj if y is None else y + yj
        x2s.append(x_ref[rows, :] + mod_ref[5:6, :] * _rms(y, g_ref[3:4, :]))
    outs = _ffn(x2s, mod_ref[6:7, :], mod_ref[7:8, :], mod_ref[8:9, :],
                g_ref[4:5, :], g_ref[5:6, :], w1_ref, w3_ref, w2_ref)
    for rows, out in zip(tiles, outs):
        o_ref[rows, :] = out


def _back(x, ys, mod, mod_row0, rows_per_batch, layer, g, wo, w1, w3, w2):
    rows, d = x.shape
    tm = BACK_ROW_TILE
    tiles_per_batch = rows_per_batch // tm
    row_tile = lambda w: pl.BlockSpec((tm, w), lambda i: (i, 0))
    in_specs = ([row_tile(d)] + [row_tile(GROUP_W)] * 4
                + [_mod_spec(layer, mod_row0, tiles_per_batch),
                   _resident(g.shape[1:], (layer,)), _resident(wo.shape[1:], (layer,)),
                   _resident(w1.shape[2:], (layer, 1)), _resident(w3.shape[2:], (layer, 1)),
                   _resident(w2.shape[2:], (layer, 1))])
    return pl.pallas_call(
        _back_kernel, grid=(rows // tm,), in_specs=in_specs,
        out_specs=row_tile(d), out_shape=jax.ShapeDtypeStruct((rows, d), F32),
        compiler_params=_params(1), name="back",
    )(x, *ys, mod, g, wo, w1, w3, w2)


def _conv_kernel(prev_ref, cur_ref, next_ref, w_ref, p_ref, o_ref, pad_ref, *, n_chunks, rc, sub):
    c = pl.program_id(1)

    def glu(u):
        return u[:, :GROUP_W] * _sigmoid(u[:, GROUP_W:])

    pad_ref[0:CONV_HALO, :] = jnp.where(c > 0, glu(prev_ref[...]), 0.0)
    pad_ref[CONV_HALO:CONV_HALO + rc, :] = glu(cur_ref[...])
    pad_ref[CONV_HALO + rc:, :] = jnp.where(c < n_chunks - 1, glu(next_ref[...]), 0.0)
    bias, ln_g, ln_b = p_ref[0:1, :], p_ref[1:2, :], p_ref[2:3, :]
    lead = CONV_HALO - CONV_K // 2
    zrows = sub + V7X_SUBLANES
    for s in range(rc // sub):
        acc = jnp.zeros((sub, GROUP_W), F32) + bias
        for b in range(V7X_SUBLANES):
            z = None
            for a in range((lead + CONV_K - 1) // V7X_SUBLANES + 1):
                k = V7X_SUBLANES * a + b - lead
                if 0 <= k < CONV_K:
                    r0 = s * sub + V7X_SUBLANES * a
                    term = pad_ref[r0:r0 + zrows, :] * w_ref[k:k + 1, :]
                    z = term if z is None else z + term
            acc = acc + z[b:b + sub, :]
        mu = jnp.mean(acc, axis=-1, keepdims=True)
        cen = acc - mu
        var = jnp.mean(cen * cen, axis=-1, keepdims=True)
        yn = cen * lax.rsqrt(var + EPS) * ln_g + ln_b
        o_ref[s * sub:(s + 1) * sub, :] = (yn * _sigmoid(yn)).astype(BF16)


def _conv(ua, conv_w, conv_p, layer, rows_per_batch):
    rows = ua.shape[0]
    batch = rows // rows_per_batch
    rc = 256
    n_chunks = rows_per_batch // rc
    hb = rc // CONV_HALO
    n_halo = rows // CONV_HALO
    cur = pl.BlockSpec((rc, 512), lambda b, c: (b * n_chunks + c, 0))
    prev = pl.BlockSpec((CONV_HALO, 512), lambda b, c: (jnp.maximum((b * n_chunks + c) * hb - 1, 0), 0))
    nxt = pl.BlockSpec((CONV_HALO, 512), lambda b, c: (jnp.minimum((b * n_chunks + c + 1) * hb, n_halo - 1), 0))
    return pl.pallas_call(
        functools.partial(_conv_kernel, n_chunks=n_chunks, rc=rc, sub=64),
        grid=(batch, n_chunks),
        in_specs=[prev, cur, nxt, _resident(conv_w.shape[1:], (layer,)), _resident(conv_p.shape[1:], (layer,))],
        out_specs=pl.BlockSpec((rc, GROUP_W), lambda b, c: (b * n_chunks + c, 0)),
        out_shape=jax.ShapeDtypeStruct((rows, GROUP_W), BF16),
        scratch_shapes=[pltpu.VMEM((rc + 2 * CONV_HALO, GROUP_W), F32)],
        compiler_params=_params(2), name="conv",
    )(ua, ua, ua, conv_w, conv_p)


FOURIER_BATCHES_PER_STEP = 2


def _fourier_kernel(z_ref, wc_ref, m_ref, o_ref, *, n, scale):
    seqs = [slice(i * n, (i + 1) * n) for i in range(FOURIER_BATCHES_PER_STEP)]
    ts = [jnp.dot(z_ref[rows, :], wc_ref[...], preferred_element_type=F32) for rows in seqs]
    zzs = [jnp.concatenate([t[:, :GROUP_W], t[:, GROUP_W:]], axis=0).astype(BF16) for t in ts]
    ys = [jnp.dot(m_ref[...], zz, preferred_element_type=F32) for zz in zzs]
    for rows, y in zip(seqs, ys):
        o_ref[rows, :] = (y * scale).astype(BF16)


def _unit_circle(n, period):
    i = jnp.arange(n, dtype=jnp.int32)[:, None]
    j = jnp.arange(period, dtype=jnp.int32)[None, :]
    ang = (2.0 * np.pi / period) * ((i * j) % period).astype(F32)
    return jnp.cos(ang), jnp.sin(ang)


def _dft_tables(n):
    hi = max(n // GRID_W, 1)
    lo = n // hi
    k = jnp.arange(n, dtype=jnp.int32)[None, :]
    n1 = jnp.arange(hi, dtype=jnp.int32)[:, None]
    n0 = jnp.arange(lo, dtype=jnp.int32)[:, None]
    a1 = (2.0 * np.pi / hi) * ((n1 * k) % hi).astype(F32)
    a0 = (2.0 * np.pi / n) * ((n0 * k) % n).astype(F32)
    c1, s1, c0, s0 = jnp.cos(a1)[:, None], jnp.sin(a1)[:, None], jnp.cos(a0)[None], jnp.sin(a0)[None]
    cos = (c1 * c0 - s1 * s0).reshape(n, n)
    sin = (s1 * c0 + c1 * s0).reshape(n, n)
    m = jnp.concatenate([cos, -sin], axis=1).astype(BF16)
    cc, sc = _unit_circle(HEAD_DIM, HEAD_DIM)
    eye = jnp.eye(GROUP_W // HEAD_DIM, dtype=F32)
    wc = jnp.concatenate([jnp.kron(eye, cc), jnp.kron(eye, sc)], axis=1).astype(BF16)
    return wc, m


def _fourier(ub, wc, m, rows_per_batch):
    rows = ub.shape[0]
    n = rows_per_batch
    step_rows = FOURIER_BATCHES_PER_STEP * n
    blk = pl.BlockSpec((step_rows, GROUP_W), lambda b: (b, 0))
    return pl.pallas_call(
        functools.partial(_fourier_kernel, n=n, scale=float((n * HEAD_DIM) ** -0.5)),
        grid=(rows // step_rows,), in_specs=[blk, _resident(wc.shape), _resident(m.shape)], out_specs=blk,
        out_shape=jax.ShapeDtypeStruct((rows, GROUP_W), BF16),
        compiler_params=_params(1), name="fourier",
    )(ub, wc, m)


def _head_mask(rows_per_head, heads=N_HEADS, head0=0):
    shape = (heads * rows_per_head, GROUP_W)
    row_h = lax.broadcasted_iota(jnp.int32, shape, 0) // rows_per_head + head0
    lane_h = lax.broadcasted_iota(jnp.int32, shape, 1) // HEAD_DIM
    return (row_h == lane_h).astype(F32)


def _stack_heads(q, mask_bf16):
    heads = mask_bf16.shape[0] // q.shape[0]
    return jnp.concatenate([q] * heads, axis=0) * mask_bf16


def _unstack_heads(o, rows_per_head, head0=0):
    heads = o.shape[0] // rows_per_head
    assert heads % 2 == 0 and head0 % 2 == 0
    first = _lane_index((rows_per_head, V7X_LANES)) < HEAD_DIM
    tiles = []
    for pair in range(heads // 2):
        lanes = slice((head0 // 2 + pair) * V7X_LANES, (head0 // 2 + pair + 1) * V7X_LANES)
        lo = o[2 * pair * rows_per_head:(2 * pair + 1) * rows_per_head, lanes]
        hi = o[(2 * pair + 1) * rows_per_head:(2 * pair + 2) * rows_per_head, lanes]
        tiles.append(jnp.where(first, lo, hi))
    return tiles[0] if len(tiles) == 1 else jnp.concatenate(tiles, axis=1)


def _dot_nt(a, b):
    return lax.dot_general(a, b, (((1,), (1,)), ((), ())), preferred_element_type=F32)


def _tree(op, xs):
    while len(xs) > 1:
        xs = [op(xs[i], xs[i + 1]) if i + 1 < len(xs) else xs[i] for i in range(0, len(xs), 2)]
    return xs[0]


def _lane_tiles(*arrays):
    return [a[:, i:i + V7X_LANES] for a in arrays for i in range(0, a.shape[1], V7X_LANES)]


def _ctx_dense(cq_ref, kx, vx, sink_ref, oc_ref):
    hpc = 2
    outs = []
    for g in range(N_HEADS // hpc):
        qst = _stack_heads(cq_ref[...], _head_mask(CTX_LEN, hpc, g * hpc).astype(BF16))
        s = _dot_nt(qst, kx)
        m = jnp.max(_tree(jnp.maximum, _lane_tiles(s)), axis=-1, keepdims=True)
        sink = None if sink_ref is None else sink_ref[g * hpc * CTX_LEN:(g + 1) * hpc * CTX_LEN, :]
        if sink is not None:
            m = jnp.maximum(m, sink)
        p = jnp.exp2(s - m)
        den = jnp.sum(_tree(jnp.add, _lane_tiles(p)), axis=-1, keepdims=True)
        if sink is not None:
            den = den + jnp.exp2(sink - m)
        o = jnp.dot(p.astype(BF16), vx, preferred_element_type=F32) / den
        outs.append(_unstack_heads(o, CTX_LEN, g * hpc))
    oc_ref[...] = jnp.concatenate(outs, axis=1).astype(BF16)


NAT_ROWS_PER_STEP = 2


def _pipelined_loop(n_steps, scores_fn, softmax_fn, pv_fn):
    scores_fn(0, 0)

    def pair(it, carry):
        for slot in (0, 1):
            step = 2 * it + slot
            scores_fn(jnp.minimum(step + 1, n_steps - 1), 1 - slot)
            softmax_fn(slot)
            pv_fn(step, slot)
        return carry

    lax.fori_loop(0, n_steps // 2, pair, 0)


def _pipeline_scratch(chains, rows, n_loc):
    kinds = [((chains, rows, n_loc), F32), ((chains, rows, CTX_LEN), F32),
             ((chains, rows, n_loc), BF16), ((chains, rows, CTX_LEN), BF16), ((chains, rows, 1), F32)]
    return tuple(pltpu.VMEM(shape, dtype) for shape, dtype in kinds for _ in range(2))


def _softmax_stage(scratch, slot, chain, sink):
    sl, sc, pl_, pc, dn = (scratch[2 * i + slot] for i in range(5))
    s, c = sl[chain], sc[chain]
    m = jnp.max(_tree(jnp.maximum, _lane_tiles(s, c)), axis=-1, keepdims=True)
    if sink is not None:
        m = jnp.maximum(m, sink)
    p = jnp.exp2(s - m)
    q = jnp.exp2(c - m)
    den = jnp.sum(_tree(jnp.add, _lane_tiles(p, q)), axis=-1, keepdims=True)
    if sink is not None:
        den = den + jnp.exp2(sink - m)
    pl_[chain] = p.astype(BF16)
    pc[chain] = q.astype(BF16)
    dn[chain] = den


def _pv_stage(scratch, slot, chain, vb, vx):
    pl_, pc, dn = (scratch[2 * i + slot] for i in (2, 3, 4))
    o = (jnp.dot(pl_[chain], vb, preferred_element_type=F32) + jnp.dot(pc[chain], vx, preferred_element_type=F32))
    return o / dn[chain]


def _nat_stages(q_ref, k_ref, v_ref, kx_ref, vx_ref, bias_ref, o_ref, scratch):
    mask_b = _head_mask(GRID_W).astype(BF16)
    band = NAT_ROWS * GRID_W
    rps = NAT_ROWS_PER_STEP

    def geometry(r):
        rs = jnp.clip(r - NAT_ROWS // 2, 0, GRID_H - NAT_ROWS)
        return pl.multiple_of(r * GRID_W, GRID_W), pl.multiple_of(rs * GRID_W, GRID_W), rs - r + NAT_ROWS - 1

    def scores(step, slot):
        for j in range(rps):
            q0, k0, d0 = geometry(step * rps + j)
            qst = _stack_heads(q_ref[pl.ds(q0, GRID_W), :], mask_b)
            bias = jnp.concatenate([bias_ref[d0 + 2 * a] for a in range(NAT_ROWS // 2)], axis=1)
            scratch[slot][j] = _dot_nt(qst, k_ref[pl.ds(k0, band), :]) + bias
            scratch[2 + slot][j] = _dot_nt(qst, kx_ref[...])

    def softmax(slot):
        for j in range(rps):
            _softmax_stage(scratch, slot, j, None)

    def pv(step, slot):
        for j in range(rps):
            q0, k0, _ = geometry(step * rps + j)
            o = _pv_stage(scratch, slot, j, v_ref[pl.ds(k0, band), :], vx_ref[...])
            o_ref[pl.ds(q0, GRID_W), :] = _unstack_heads(o, GRID_W).astype(BF16)

    return scores, softmax, pv


def _nat_bias_table(rel_bias):
    depth = rel_bias.shape[0]
    w = GRID_W
    n_dr = 2 * NAT_ROWS - 1
    lo = w - NAT_COLS
    g = jnp.pad(rel_bias.astype(F32) * LOG2_E, ((0, 0),) * 3 + ((lo, 2 * w - lo - (2 * NAT_COLS - 1)),))
    flat = jnp.tile(g, (1, 1, 1, w))[..., :w * (2 * w - 1)]
    t = flat.reshape(depth, N_HEADS, n_dr, w, 2 * w - 1)[..., w - 1:]
    qc = np.arange(w)[:, None]
    kc = np.arange(w)[None, :]
    ws = np.clip(qc - NAT_COLS // 2, 0, w - NAT_COLS)
    col_ok = (kc >= ws) & (kc < ws + NAT_COLS)
    t = jnp.where(col_ok, t, NEG).transpose(0, 2, 1, 3, 4).reshape(depth, n_dr, N_HEADS * w, w)
    return jnp.concatenate([t[:, :-1], t[:, 1:]], axis=-1)


SWA_HEADS_PER_CHAIN = 2


def _swa_stages(q_ref, k_ref, v_ref, kx_ref, vx_ref, sink_ref, win_ref, o_ref, scratch):
    hpc = SWA_HEADS_PER_CHAIN
    chain_rows = hpc * SWA_BLOCK
    band = 3 * SWA_BLOCK
    n_blocks = SEQ // SWA_BLOCK

    @pl.when(pl.program_id(0) == 0)
    def _():
        shape = (chain_rows, band)
        rel = (lax.broadcasted_iota(jnp.int32, shape, 1)
               - lax.broadcasted_iota(jnp.int32, shape, 0) % SWA_BLOCK)
        for i, shift in enumerate((0, -SWA_BLOCK, -2 * SWA_BLOCK)):
            win_ref[i] = jnp.where(jnp.abs(rel + shift) <= SWA_WINDOW, 0.0, NEG)

    masks_b = [_head_mask(SWA_BLOCK, hpc, g * hpc).astype(BF16) for g in range(N_HEADS // hpc)]

    def geometry(nb):
        q0 = pl.multiple_of(nb * SWA_BLOCK, SWA_BLOCK)
        return q0, pl.multiple_of(jnp.clip(q0 - SWA_BLOCK, 0, SEQ - band), SWA_BLOCK)

    def scores(nb, slot):
        q0, k0 = geometry(nb)
        placement = jnp.where(nb == 0, 0, jnp.where(nb == n_blocks - 1, 2, 1))
        q = q_ref[pl.ds(q0, SWA_BLOCK), :]
        for g in range(len(masks_b)):
            qst = _stack_heads(q, masks_b[g])
            scratch[slot][g] = _dot_nt(qst, k_ref[pl.ds(k0, band), :]) + win_ref[placement]
            scratch[2 + slot][g] = _dot_nt(qst, kx_ref[...])

    def softmax(slot):
        for g in range(len(masks_b)):
            _softmax_stage(scratch, slot, g, sink_ref[g * chain_rows:(g + 1) * chain_rows, :])

    def pv(nb, slot):
        q0, k0 = geometry(nb)
        outs = []
        for g in range(len(masks_b)):
            o = _pv_stage(scratch, slot, g, v_ref[pl.ds(k0, band), :], vx_ref[...])
            outs.append(_unstack_heads(o, SWA_BLOCK, g * hpc))
        o_ref[pl.ds(q0, SWA_BLOCK), :] = jnp.concatenate(outs, axis=1).astype(BF16)

    return scores, softmax, pv


ATTENTION_STEPS = GRID_H // NAT_ROWS_PER_STEP
assert ATTENTION_STEPS == SEQ // SWA_BLOCK


def _attention_kernel(*refs, ctx_out):
    refs = list(refs)
    take = lambda n: [refs.pop(0) for _ in range(n)]
    nq, nk, nv, nkx, nvx, sq, sk, sv, skx, svx, bias_ref, sink_ref, csink_ref = take(13)
    cqn, cqs = take(2) if ctx_out else (None, None)
    o_nat, o_swa = take(2)
    oc_nat, oc_swa = take(2) if ctx_out else (None, None)
    (win_ref,), nat_scratch, swa_scratch = take(1), take(10), take(10)
    mixers = [_nat_stages(nq, nk, nv, nkx, nvx, bias_ref, o_nat, nat_scratch),
              _swa_stages(sq, sk, sv, skx, svx, sink_ref, win_ref, o_swa, swa_scratch)]

    def stage(i):
        return lambda *args: [mixer[i](*args) for mixer in mixers]

    _pipelined_loop(ATTENTION_STEPS, stage(0), stage(1), stage(2))
    if ctx_out:
        _ctx_dense(cqn, nkx[...], nvx[...], None, oc_nat)
        _ctx_dense(cqs, skx[...], svx[...], csink_ref, oc_swa)


def _attention(nat_lat, nat_ctx, swa_lat, swa_ctx, nat_bias, sink_lat, sink_ctx, layer, ctx_out):
    batch = nat_lat[0].shape[0] // SEQ
    lat_blk = pl.BlockSpec((SEQ, GROUP_W), lambda b: (b, 0))
    ctx_blk = pl.BlockSpec((CTX_LEN, GROUP_W), lambda b: (b, 0))
    consts = (nat_bias, sink_lat, sink_ctx)
    in_specs = ([lat_blk] * 3 + [ctx_blk] * 2) * 2 + [_resident(c.shape[1:], (layer,)) for c in consts]
    args = list(nat_lat) + list(nat_ctx[1:]) + list(swa_lat) + list(swa_ctx[1:]) + list(consts)
    out_shape = [jax.ShapeDtypeStruct((batch * SEQ, GROUP_W), BF16)] * 2
    out_specs = [lat_blk] * 2
    if ctx_out:
        in_specs += [ctx_blk] * 2
        args += [nat_ctx[0], swa_ctx[0]]
        out_shape += [jax.ShapeDtypeStruct((batch * CTX_LEN, GROUP_W), BF16)] * 2
        out_specs += [ctx_blk] * 2
    swa_rows, swa_chains = SWA_HEADS_PER_CHAIN * SWA_BLOCK, N_HEADS // SWA_HEADS_PER_CHAIN
    scratch = ((pltpu.VMEM((3, swa_rows, 3 * SWA_BLOCK), F32),)
               + _pipeline_scratch(NAT_ROWS_PER_STEP, N_HEADS * GRID_W, NAT_ROWS * GRID_W)
               + _pipeline_scratch(swa_chains, swa_rows, 3 * SWA_BLOCK))
    outs = pl.pallas_call(
        functools.partial(_attention_kernel, ctx_out=ctx_out), grid=(batch,), in_specs=in_specs,
        out_specs=out_specs, out_shape=out_shape, scratch_shapes=list(scratch),
        compiler_params=_params(1), name="attention",
    )(*args)
    return tuple(outs) if ctx_out else (outs[0], outs[1], None, None)


def _rope_tables():
    t = np.arange(SEQ)
    pos = np.stack([t // GRID_W, t % GRID_W], axis=1).astype(np.float32)
    half = HEAD_DIM // 4
    inv = ROPE_BASE ** (-jnp.arange(half, dtype=F32) / half)
    ang = jnp.asarray(pos)[:, :, None] * inv[None, None, :]
    cos = jnp.cos(ang)
    sin = jnp.sin(ang)
    cos64 = jnp.concatenate([cos, cos], axis=-1).reshape(SEQ, HEAD_DIM)
    sin64 = jnp.concatenate([-sin, sin], axis=-1).reshape(SEQ, HEAD_DIM)
    return jnp.tile(cos64, (1, V7X_LANES // HEAD_DIM)), jnp.tile(sin64, (1, V7X_LANES // HEAD_DIM))


def kernel(x, c, ctx, c_ctx, w_ada, b_ada, norm_g, ffn_w1, ffn_w3, ffn_w2, w_in, conv_w, conv_b, conv_ln_g,
           conv_ln_b, nat_rel_bias, sink_logits, w_out):
    bsz, seq, d = x.shape
    depth = w_ada.shape[0]
    assert (seq, d, ctx.shape[1]) == (SEQ, D_MODEL, CTX_LEN) and bsz < MOD_ROWS
    xl = x.reshape(bsz * seq, d)
    xc = ctx.reshape(bsz * CTX_LEN, d)

    cv = jnp.zeros((MOD_ROWS, d), F32).at[:bsz].set(c).at[bsz].set(c_ctx)
    mod = _ada_mod(cv, w_ada, b_ada).reshape(depth, MOD_ROWS, N_MOD, d)
    rope_tabs = _rope_tables()
    dft_lat = _dft_tables(SEQ)
    dft_ctx = _dft_tables(CTX_LEN)
    w1, w3, w2, wx, wo = (_to_bf16(w) for w in (ffn_w1, ffn_w3, ffn_w2, w_in, w_out))
    conv_p = jnp.stack([conv_b, conv_ln_g, conv_ln_b], axis=1)
    nat_bias = _nat_bias_table(nat_rel_bias)
    sink_lat = jnp.repeat(sink_logits * LOG2_E, SWA_BLOCK, axis=1)[..., None]
    sink_ctx = jnp.repeat(sink_logits * LOG2_E, CTX_LEN, axis=1)[..., None]

    for l in range(depth):
        last = l == depth - 1
        xl1, ua, ub, qn, kn, vn, qs, ks, vs = _front(xl, mod, 0, SEQ, l, norm_g, w1, w3, w2, wx, rope_tabs)
        xc1, ca, cb, cqn, ckn, cvn, cqs, cks, cvs = _front(xc, mod, bsz, bsz * CTX_LEN, l, norm_g, w1, w3, w2,
                                                           wx, None)
        y_a = _conv(ua, conv_w, conv_p, l, SEQ)
        y_b = _fourier(ub, *dft_lat, SEQ)
        y_c, y_d, yc_c, yc_d = _attention((qn, kn, vn), (cqn, ckn, cvn), (qs, ks, vs), (cqs, cks, cvs),
                                          nat_bias, sink_lat, sink_ctx, l, not last)
        xl = _back(xl1, (y_a, y_b, y_c, y_d), mod, 0, SEQ, l, norm_g, wo, w1, w3, w2)
        if not last:
            yc_a = _conv(ca, conv_w, conv_p, l, CTX_LEN)
            yc_b = _fourier(cb, *dft_ctx, CTX_LEN)
            xc = _back(xc1, (yc_a, yc_b, yc_c, yc_d), mod, bsz, bsz * CTX_LEN, l, norm_g, wo, w1, w3, w2)
    return xl.reshape(bsz, seq, d)
```

```python
import functools

import numpy as np
import jax
import jax.numpy as jnp
from jax import lax
from jax.experimental import pallas as pl
from jax.experimental.pallas import tpu as pltpu

F32 = jnp.float32
BF16 = jnp.bfloat16

D_MODEL = 1024
SEQ = 2048
CTX_LEN = 256
GRID_W = 64
GRID_H = SEQ // GRID_W
GROUP_W = 256
HEAD_DIM = 64
N_HEADS = 4
CONV_K = 31
NAT_ROWS = 8
NAT_COLS = 16
SWA_KV_HEADS = 2
SWA_WINDOW = 128
SWA_BLOCK = 128
ROPE_BASE = 10000.0
FFN_DIM = 2816
MACARON_W = 0.5
N_MOD = 9
MOD_ROWS = 24
EPS = 1e-6
NEG = -1e30
LOG2_E = float(np.log2(np.e))
Q_SCALE = HEAD_DIM ** -0.5 * LOG2_E

V7X_LANES = 128
V7X_SUBLANES = 8
V7X_MXU_DIM = 256
V7X_VMEM_LIMIT = 56 * 1024 * 1024
FRONT_ROW_TILE = 512
BACK_ROW_TILE = 1024
SUB_TILE_ROWS = 256
FFN_CHUNK_BOUNDS = (0, 6 * V7X_MXU_DIM, FFN_DIM)
CONV_HALO = 2 * V7X_SUBLANES

COL_UA = 0
COL_UB = 512
COL_QN, COL_KN, COL_VN = 768, 1024, 1280
COL_QS, COL_KS, COL_VS = 1536, 1792, 1920
IN_DIM = 2048


def _resident(shape, lead=()):
    nd = len(shape)
    return pl.BlockSpec((None,) * len(lead) + tuple(shape), lambda *_: tuple(lead) + (0,) * nd,
                        pipeline_mode=pl.Buffered(1))


def _params(n_axes=1):
    return pltpu.CompilerParams(dimension_semantics=("arbitrary",) * n_axes, vmem_limit_bytes=V7X_VMEM_LIMIT)


def _rms(x, g):
    return x * lax.rsqrt(jnp.mean(x * x, axis=-1, keepdims=True) + EPS) * g


def _sigmoid(x):
    return 1.0 / (1.0 + jnp.exp(-x))


def _sub_tiles(rows):
    assert rows % SUB_TILE_ROWS == 0
    return [slice(r0, r0 + SUB_TILE_ROWS) for r0 in range(0, rows, SUB_TILE_ROWS)]


def _ffn(xs, shift, scale, gate, g_pre, g_post, w1_ref, w3_ref, w2_ref):
    hs = [(_rms(x, g_pre) * (1.0 + scale) + shift).astype(BF16) for x in xs]
    ys = [None] * len(xs)
    for lo, hi in zip(FFN_CHUNK_BOUNDS[:-1], FFN_CHUNK_BOUNDS[1:]):
        ts = []
        for h in hs:
            a = jnp.dot(h, w1_ref[:, lo:hi], preferred_element_type=F32)
            b = jnp.dot(h, w3_ref[:, lo:hi], preferred_element_type=F32)
            ts.append((a * _sigmoid(a) * b).astype(BF16))
        for i, t in enumerate(ts):
            yc = jnp.dot(t, w2_ref[lo:hi, :], preferred_element_type=F32)
            ys[i] = yc if ys[i] is None else ys[i] + yc
    return [x + (MACARON_W * gate) * _rms(y, g_post) for x, y in zip(xs, ys)]


def _ada_kernel(cv_ref, w_ref, b_ref, o_ref):
    c = cv_ref[...]
    sc = (c * _sigmoid(c)).astype(BF16)
    o_ref[...] = jnp.dot(sc, w_ref[...].astype(BF16), preferred_element_type=F32) + b_ref[...]


def _ada_mod(cv, w_ada, b_ada):
    depth, d, n = w_ada.shape
    rows = cv.shape[0]
    tn = 9 * V7X_LANES
    return pl.pallas_call(
        _ada_kernel,
        grid=(depth, n // tn),
        in_specs=[pl.BlockSpec((rows, d), lambda l, j: (0, 0)),
                  pl.BlockSpec((None, d, tn), lambda l, j: (l, 0, j)),
                  pl.BlockSpec((None, 1, tn), lambda l, j: (l, 0, j))],
        out_specs=pl.BlockSpec((None, rows, tn), lambda l, j: (l, 0, j)),
        out_shape=jax.ShapeDtypeStruct((depth, rows, n), F32),
        compiler_params=_params(2),
        name="ada_mod",
    )(cv, w_ada, b_ada.reshape(depth, 1, n))


def _lane_index(shape):
    return lax.broadcasted_iota(jnp.int32, shape, 1)


def _rotate_half_partner(t):
    quarter = HEAD_DIM // 4
    up = pltpu.roll(t, V7X_LANES - quarter, axis=1)
    down = pltpu.roll(t, quarter, axis=1)
    return jnp.where(_lane_index(t.shape) % (2 * quarter) < quarter, up, down)


def _per_query_head(t):
    swapped = pltpu.roll(t, HEAD_DIM, axis=1)
    first = _lane_index(t.shape) < HEAD_DIM
    return jnp.concatenate([jnp.where(first, t, swapped), jnp.where(first, swapped, t)], axis=1)


def _front_kernel(*refs, rope):
    if rope:
        (x_ref, mod_ref, g_ref, w1_ref, w3_ref, w2_ref, wx_ref, cos_ref, sin_ref,
         x1_ref, ua_ref, ub_ref, qn_ref, kn_ref, vn_ref, qs_ref, ks_ref, vs_ref) = refs
    else:
        (x_ref, mod_ref, g_ref, w1_ref, w3_ref, w2_ref, wx_ref,
         x1_ref, ua_ref, ub_ref, qn_ref, kn_ref, vn_ref, qs_ref, ks_ref, vs_ref) = refs
    tiles = _sub_tiles(x_ref.shape[0])
    x1s = _ffn([x_ref[rows, :] for rows in tiles], mod_ref[0:1, :], mod_ref[1:2, :], mod_ref[2:3, :],
               g_ref[0:1, :], g_ref[1:2, :], w1_ref, w3_ref, w2_ref)
    hs = []
    for rows, x1 in zip(tiles, x1s):
        x1_ref[rows, :] = x1
        hs.append((_rms(x1, g_ref[2:3, :]) * (1.0 + mod_ref[4:5, :]) + mod_ref[3:4, :]).astype(BF16))
    us = [jnp.dot(h, wx_ref[...], preferred_element_type=F32) for h in hs]
    for rows, u in zip(tiles, us):
        ua_ref[rows, :] = u[:, COL_UA:COL_UA + 512]
        ub_ref[rows, :] = u[:, COL_UB:COL_UB + 256].astype(BF16)
        qn_ref[rows, :] = (u[:, COL_QN:COL_QN + 256] * Q_SCALE).astype(BF16)
        kn_ref[rows, :] = u[:, COL_KN:COL_KN + 256].astype(BF16)
        vn_ref[rows, :] = u[:, COL_VN:COL_VN + 256].astype(BF16)
        q_tiles = [u[:, COL_QS + i * V7X_LANES:COL_QS + (i + 1) * V7X_LANES] for i in range(2)]
        ks = u[:, COL_KS:COL_KS + V7X_LANES]
        if rope:
            cos, sin = cos_ref[rows, :], sin_ref[rows, :]
            q_tiles = [t * cos + _rotate_half_partner(t) * sin for t in q_tiles]
            ks = ks * cos + _rotate_half_partner(ks) * sin
        qs_ref[rows, :] = (jnp.concatenate(q_tiles, axis=1) * Q_SCALE).astype(BF16)
        ks_ref[rows, :] = _per_query_head(ks).astype(BF16)
        vs_ref[rows, :] = _per_query_head(u[:, COL_VS:COL_VS + V7X_LANES]).astype(BF16)


def _mod_spec(layer, row0, tiles_per_batch):
    return pl.BlockSpec((None, None, N_MOD, D_MODEL), lambda i: (layer, row0 + i // tiles_per_batch, 0, 0))


def _front(x, mod, mod_row0, rows_per_batch, layer, g, w1, w3, w2, wx, rope_tabs):
    rows, d = x.shape
    tm = FRONT_ROW_TILE
    tiles_per_batch = rows_per_batch // tm
    rope = rope_tabs is not None
    row_tile = lambda w: pl.BlockSpec((tm, w), lambda i: (i, 0))
    in_specs = [row_tile(d), _mod_spec(layer, mod_row0, tiles_per_batch),
                _resident(g.shape[1:], (layer,)), _resident(w1.shape[2:], (layer, 0)),
                _resident(w3.shape[2:], (layer, 0)), _resident(w2.shape[2:], (layer, 0)),
                _resident(wx.shape[1:], (layer,))]
    args = [x, mod, g, w1, w3, w2, wx]
    if rope:
        tab = pl.BlockSpec((tm, V7X_LANES), lambda i: (i % tiles_per_batch, 0))
        in_specs += [tab, tab]
        args += list(rope_tabs)
    out_shape = ([jax.ShapeDtypeStruct((rows, d), F32), jax.ShapeDtypeStruct((rows, 512), F32)]
                 + [jax.ShapeDtypeStruct((rows, 256), BF16)] * 7)
    out_specs = [row_tile(d), row_tile(512)] + [row_tile(256)] * 7
    return pl.pallas_call(
        functools.partial(_front_kernel, rope=rope),
        grid=(rows // tm,), in_specs=in_specs, out_specs=out_specs, out_shape=out_shape,
        compiler_params=_params(1), name="front_rope" if rope else "front_ctx",
    )(*args)


def _back_kernel(x_ref, ya_ref, yb_ref, yc_ref, yd_ref, mod_ref, g_ref, wo_ref, w1_ref, w3_ref, w2_ref, o_ref):
    tiles = _sub_tiles(x_ref.shape[0])
    x2s = []
    for rows in tiles:
        y = None
        for j, y_ref in enumerate((ya_ref, yb_ref, yc_ref, yd_ref)):
            yj = jnp.dot(y_ref[rows, :], wo_ref[j * GROUP_W:(j + 1) * GROUP_W, :], preferred_element_type=F32)
            y = yj if y is None else y + yj
        x2s.append(x_ref[rows, :] + mod_ref[5:6, :] * _rms(y, g_ref[3:4, :]))
    outs = _ffn(x2s, mod_ref[6:7, :], mod_ref[7:8, :], mod_ref[8:9, :],
                g_ref[4:5, :], g_ref[5:6, :], w1_ref, w3_ref, w2_ref)
    for rows, out in zip(tiles, outs):
        o_ref[rows, :] = out


def _back(x, ys, mod, mod_row0, rows_per_batch, layer, g, wo, w1, w3, w2):
    rows, d = x.shape
    tm = BACK_ROW_TILE
    tiles_per_batch = rows_per_batch // tm
    row_tile = lambda w: pl.BlockSpec((tm, w), lambda i: (i, 0))
    in_specs = ([row_tile(d)] + [row_tile(GROUP_W)] * 4
                + [_mod_spec(layer, mod_row0, tiles_per_batch),
                   _resident(g.shape[1:], (layer,)), _resident(wo.shape[1:], (layer,)),
                   _resident(w1.shape[2:], (layer, 1)), _resident(w3.shape[2:], (layer, 1)),
                   _resident(w2.shape[2:], (layer, 1))])
    return pl.pallas_call(
        _back_kernel, grid=(rows // tm,), in_specs=in_specs,
        out_specs=row_tile(d), out_shape=jax.ShapeDtypeStruct((rows, d), F32),
        compiler_params=_params(1), name="back",
    )(x, *ys, mod, g, wo, w1, w3, w2)


def _conv_kernel(prev_ref, cur_ref, next_ref, w_ref, p_ref, o_ref, pad_ref, *, n_chunks, rc, sub):
    c = pl.program_id(1)

    def glu(u):
        return u[:, :GROUP_W] * _sigmoid(u[:, GROUP_W:])

    pad_ref[0:CONV_HALO, :] = jnp.where(c > 0, glu(prev_ref[...]), 0.0)
    pad_ref[CONV_HALO:CONV_HALO + rc, :] = glu(cur_ref[...])
    pad_ref[CONV_HALO + rc:, :] = jnp.where(c < n_chunks - 1, glu(next_ref[...]), 0.0)
    bias, ln_g, ln_b = p_ref[0:1, :], p_ref[1:2, :], p_ref[2:3, :]
    lead = CONV_HALO - CONV_K // 2
    zrows = sub + V7X_SUBLANES
    for s in range(rc // sub):
        acc = jnp.zeros((sub, GROUP_W), F32) + bias
        for b in range(V7X_SUBLANES):
            z = None
            for a in range((lead + CONV_K - 1) // V7X_SUBLANES + 1):
                k = V7X_SUBLANES * a + b - lead
                if 0 <= k < CONV_K:
                    r0 = s * sub + V7X_SUBLANES * a
                    term = pad_ref[r0:r0 + zrows, :] * w_ref[k:k + 1, :]
                    z = term if z is None else z + term
            acc = acc + z[b:b + sub, :]
        mu = jnp.mean(acc, axis=-1, keepdims=True)
        cen = acc - mu
        var = jnp.mean(cen * cen, axis=-1, keepdims=True)
        yn = cen * lax.rsqrt(var + EPS) * ln_g + ln_b
        o_ref[s * sub:(s + 1) * sub, :] = (yn * _sigmoid(yn)).astype(BF16)


def _conv(ua, conv_w, conv_p, layer, rows_per_batch):
    rows = ua.shape[0]
    batch = rows // rows_per_batch
    rc = 256
    n_chunks = rows_per_batch // rc
    hb = rc // CONV_HALO
    n_halo = rows // CONV_HALO
    cur = pl.BlockSpec((rc, 512), lambda b, c: (b * n_chunks + c, 0))
    prev = pl.BlockSpec((CONV_HALO, 512), lambda b, c: (jnp.maximum((b * n_chunks + c) * hb - 1, 0), 0))
    nxt = pl.BlockSpec((CONV_HALO, 512), lambda b, c: (jnp.minimum((b * n_chunks + c + 1) * hb, n_halo - 1), 0))
    return pl.pallas_call(
        functools.partial(_conv_kernel, n_chunks=n_chunks, rc=rc, sub=128),
        grid=(batch, n_chunks),
        in_specs=[prev, cur, nxt, _resident(conv_w.shape[1:], (layer,)), _resident(conv_p.shape[1:], (layer,))],
        out_specs=pl.BlockSpec((rc, GROUP_W), lambda b, c: (b * n_chunks + c, 0)),
        out_shape=jax.ShapeDtypeStruct((rows, GROUP_W), BF16),
        scratch_shapes=[pltpu.VMEM((rc + 2 * CONV_HALO, GROUP_W), F32)],
        compiler_params=_params(2), name="conv",
    )(ua, ua, ua, conv_w, conv_p)


FOURIER_BATCHES_PER_STEP = 2


def _fourier_kernel(z_ref, wc_ref, m_ref, o_ref, *, n, scale):
    seqs = [slice(i * n, (i + 1) * n) for i in range(FOURIER_BATCHES_PER_STEP)]
    ts = [jnp.dot(z_ref[rows, :], wc_ref[...], preferred_element_type=F32) for rows in seqs]
    zzs = [jnp.concatenate([t[:, :GROUP_W], t[:, GROUP_W:]], axis=0).astype(BF16) for t in ts]
    ys = [jnp.dot(m_ref[...], zz, preferred_element_type=F32) for zz in zzs]
    for rows, y in zip(seqs, ys):
        o_ref[rows, :] = (y * scale).astype(BF16)


def _unit_circle(n, period):
    i = jnp.arange(n, dtype=jnp.int32)[:, None]
    j = jnp.arange(period, dtype=jnp.int32)[None, :]
    ang = (2.0 * np.pi / period) * ((i * j) % period).astype(F32)
    return jnp.cos(ang), jnp.sin(ang)


def _dft_tables(n):
    hi = max(n // GRID_W, 1)
    lo = n // hi
    k = jnp.arange(n, dtype=jnp.int32)[None, :]
    n1 = jnp.arange(hi, dtype=jnp.int32)[:, None]
    n0 = jnp.arange(lo, dtype=jnp.int32)[:, None]
    a1 = (2.0 * np.pi / hi) * ((n1 * k) % hi).astype(F32)
    a0 = (2.0 * np.pi / n) * ((n0 * k) % n).astype(F32)
    c1, s1, c0, s0 = jnp.cos(a1)[:, None], jnp.sin(a1)[:, None], jnp.cos(a0)[None], jnp.sin(a0)[None]
    cos = (c1 * c0 - s1 * s0).reshape(n, n)
    sin = (s1 * c0 + c1 * s0).reshape(n, n)
    m = jnp.concatenate([cos, -sin], axis=1).astype(BF16)
    cc, sc = _unit_circle(HEAD_DIM, HEAD_DIM)
    eye = jnp.eye(GROUP_W // HEAD_DIM, dtype=F32)
    wc = jnp.concatenate([jnp.kron(eye, cc), jnp.kron(eye, sc)], axis=1).astype(BF16)
    return wc, m


def _fourier(ub, wc, m, rows_per_batch):
    rows = ub.shape[0]
    n = rows_per_batch
    step_rows = FOURIER_BATCHES_PER_STEP * n
    blk = pl.BlockSpec((step_rows, GROUP_W), lambda b: (b, 0))
    return pl.pallas_call(
        functools.partial(_fourier_kernel, n=n, scale=float((n * HEAD_DIM) ** -0.5)),
        grid=(rows // step_rows,), in_specs=[blk, _resident(wc.shape), _resident(m.shape)], out_specs=blk,
        out_shape=jax.ShapeDtypeStruct((rows, GROUP_W), BF16),
        compiler_params=_params(1), name="fourier",
    )(ub, wc, m)


def _head_mask(rows_per_head, heads=N_HEADS, head0=0):
    shape = (heads * rows_per_head, GROUP_W)
    row_h = lax.broadcasted_iota(jnp.int32, shape, 0) // rows_per_head + head0
    lane_h = lax.broadcasted_iota(jnp.int32, shape, 1) // HEAD_DIM
    return (row_h == lane_h).astype(F32)


def _stack_heads(q, mask_bf16):
    heads = mask_bf16.shape[0] // q.shape[0]
    return jnp.concatenate([q] * heads, axis=0) * mask_bf16


def _unstack_heads(o, rows_per_head, head0=0):
    heads = o.shape[0] // rows_per_head
    assert heads % 2 == 0 and head0 % 2 == 0
    first = _lane_index((rows_per_head, V7X_LANES)) < HEAD_DIM
    tiles = []
    for pair in range(heads // 2):
        lanes = slice((head0 // 2 + pair) * V7X_LANES, (head0 // 2 + pair + 1) * V7X_LANES)
        lo = o[2 * pair * rows_per_head:(2 * pair + 1) * rows_per_head, lanes]
        hi = o[(2 * pair + 1) * rows_per_head:(2 * pair + 2) * rows_per_head, lanes]
        tiles.append(jnp.where(first, lo, hi))
    return tiles[0] if len(tiles) == 1 else jnp.concatenate(tiles, axis=1)


def _dot_nt(a, b):
    return lax.dot_general(a, b, (((1,), (1,)), ((), ())), preferred_element_type=F32)


def _tree(op, xs):
    while len(xs) > 1:
        xs = [op(xs[i], xs[i + 1]) if i + 1 < len(xs) else xs[i] for i in range(0, len(xs), 2)]
    return xs[0]


def _lane_tiles(*arrays):
    return [a[:, i:i + V7X_LANES] for a in arrays for i in range(0, a.shape[1], V7X_LANES)]


def _ctx_dense(jobs):
    hpc = 2
    chains = [(job, g) for job in jobs for g in range(N_HEADS // hpc)]
    scores = [_dot_nt(_stack_heads(cq_ref[...], _head_mask(CTX_LEN, hpc, g * hpc).astype(BF16)), kx_ref[...])
              for (cq_ref, kx_ref, _, _, _), g in chains]
    probs = []
    for ((_, _, _, sink_ref, _), g), s in zip(chains, scores):
        m = jnp.max(_tree(jnp.maximum, _lane_tiles(s)), axis=-1, keepdims=True)
        sink = None if sink_ref is None else sink_ref[g * hpc * CTX_LEN:(g + 1) * hpc * CTX_LEN, :]
        if sink is not None:
            m = jnp.maximum(m, sink)
        p = jnp.exp2(s - m)
        den = jnp.sum(_tree(jnp.add, _lane_tiles(p)), axis=-1, keepdims=True)
        if sink is not None:
            den = den + jnp.exp2(sink - m)
        probs.append((p.astype(BF16), den))
    outs = [_unstack_heads(jnp.dot(p, vx_ref[...], preferred_element_type=F32) / den, CTX_LEN, g * hpc)
            for ((_, _, vx_ref, _, _), g), (p, den) in zip(chains, probs)]
    for i, (_, _, _, _, oc_ref) in enumerate(jobs):
        oc_ref[...] = jnp.concatenate(outs[i * (N_HEADS // hpc):(i + 1) * (N_HEADS // hpc)], axis=1).astype(BF16)


NAT_ROWS_PER_STEP = 2


def _pipelined_loop(n_steps, scores_fn, softmax_fn, pv_fn):
    scores_fn(0, 0)

    def pair(it, carry):
        for slot in (0, 1):
            step = 2 * it + slot
            scores_fn(jnp.minimum(step + 1, n_steps - 1), 1 - slot)
            softmax_fn(slot)
            pv_fn(step, slot)
        return carry

    lax.fori_loop(0, n_steps // 2, pair, 0)


def _pipeline_scratch(chains, rows, n_loc):
    kinds = [((chains, rows, n_loc), F32), ((chains, rows, CTX_LEN), F32),
             ((chains, rows, n_loc), BF16), ((chains, rows, CTX_LEN), BF16), ((chains, rows, 1), F32)]
    return tuple(pltpu.VMEM(shape, dtype) for shape, dtype in kinds for _ in range(2))


def _softmax_stage(scratch, slot, chain, sink):
    sl, sc, pl_, pc, dn = (scratch[2 * i + slot] for i in range(5))
    s, c = sl[chain], sc[chain]
    m = jnp.max(_tree(jnp.maximum, _lane_tiles(s, c)), axis=-1, keepdims=True)
    if sink is not None:
        m = jnp.maximum(m, sink)
    p = jnp.exp2(s - m)
    q = jnp.exp2(c - m)
    den = jnp.sum(_tree(jnp.add, _lane_tiles(p, q)), axis=-1, keepdims=True)
    if sink is not None:
        den = den + jnp.exp2(sink - m)
    pl_[chain] = p.astype(BF16)
    pc[chain] = q.astype(BF16)
    dn[chain] = den


def _pv_stage(scratch, slot, chain, vb, vx):
    pl_, pc, dn = (scratch[2 * i + slot] for i in (2, 3, 4))
    o = (jnp.dot(pl_[chain], vb, preferred_element_type=F32) + jnp.dot(pc[chain], vx, preferred_element_type=F32))
    return o / dn[chain]


def _nat_stages(q_ref, k_ref, v_ref, kx_ref, vx_ref, bias_ref, o_ref, scratch):
    mask_b = _head_mask(GRID_W).astype(BF16)
    band = NAT_ROWS * GRID_W
    rps = NAT_ROWS_PER_STEP

    def geometry(r):
        rs = jnp.clip(r - NAT_ROWS // 2, 0, GRID_H - NAT_ROWS)
        return pl.multiple_of(r * GRID_W, GRID_W), pl.multiple_of(rs * GRID_W, GRID_W), rs - r + NAT_ROWS - 1

    def scores(step, slot):
        for j in range(rps):
            q0, k0, d0 = geometry(step * rps + j)
            qst = _stack_heads(q_ref[pl.ds(q0, GRID_W), :], mask_b)
            bias = jnp.concatenate([bias_ref[d0 + 2 * a] for a in range(NAT_ROWS // 2)], axis=1)
            scratch[slot][j] = _dot_nt(qst, k_ref[pl.ds(k0, band), :]) + bias
            scratch[2 + slot][j] = _dot_nt(qst, kx_ref[...])

    def softmax(slot):
        for j in range(rps):
            _softmax_stage(scratch, slot, j, None)

    def pv(step, slot):
        for j in range(rps):
            q0, k0, _ = geometry(step * rps + j)
            o = _pv_stage(scratch, slot, j, v_ref[pl.ds(k0, band), :], vx_ref[...])
            o_ref[pl.ds(q0, GRID_W), :] = _unstack_heads(o, GRID_W).astype(BF16)

    return scores, softmax, pv


def _nat_bias_table(rel_bias):
    depth = rel_bias.shape[0]
    w = GRID_W
    n_dr = 2 * NAT_ROWS - 1
    lo = w - NAT_COLS
    g = jnp.pad(rel_bias.astype(F32) * LOG2_E, ((0, 0),) * 3 + ((lo, 2 * w - lo - (2 * NAT_COLS - 1)),))
    flat = jnp.tile(g, (1, 1, 1, w))[..., :w * (2 * w - 1)]
    t = flat.reshape(depth, N_HEADS, n_dr, w, 2 * w - 1)[..., w - 1:]
    qc = np.arange(w)[:, None]
    kc = np.arange(w)[None, :]
    ws = np.clip(qc - NAT_COLS // 2, 0, w - NAT_COLS)
    col_ok = (kc >= ws) & (kc < ws + NAT_COLS)
    t = jnp.where(col_ok, t, NEG).transpose(0, 2, 1, 3, 4).reshape(depth, n_dr, N_HEADS * w, w)
    return jnp.concatenate([t[:, :-1], t[:, 1:]], axis=-1)


SWA_HEADS_PER_CHAIN = 2


def _swa_stages(q_ref, k_ref, v_ref, kx_ref, vx_ref, sink_ref, win_ref, o_ref, scratch):
    hpc = SWA_HEADS_PER_CHAIN
    chain_rows = hpc * SWA_BLOCK
    band = 3 * SWA_BLOCK
    n_blocks = SEQ // SWA_BLOCK

    @pl.when(pl.program_id(0) == 0)
    def _():
        shape = (chain_rows, band)
        rel = (lax.broadcasted_iota(jnp.int32, shape, 1)
               - lax.broadcasted_iota(jnp.int32, shape, 0) % SWA_BLOCK)
        for i, shift in enumerate((0, -SWA_BLOCK, -2 * SWA_BLOCK)):
            win_ref[i] = jnp.where(jnp.abs(rel + shift) <= SWA_WINDOW, 0.0, NEG)

    masks_b = [_head_mask(SWA_BLOCK, hpc, g * hpc).astype(BF16) for g in range(N_HEADS // hpc)]

    def geometry(nb):
        q0 = pl.multiple_of(nb * SWA_BLOCK, SWA_BLOCK)
        return q0, pl.multiple_of(jnp.clip(q0 - SWA_BLOCK, 0, SEQ - band), SWA_BLOCK)

    def scores(nb, slot):
        q0, k0 = geometry(nb)
        placement = jnp.where(nb == 0, 0, jnp.where(nb == n_blocks - 1, 2, 1))
        q = q_ref[pl.ds(q0, SWA_BLOCK), :]
        for g in range(len(masks_b)):
            qst = _stack_heads(q, masks_b[g])
            scratch[slot][g] = _dot_nt(qst, k_ref[pl.ds(k0, band), :]) + win_ref[placement]
            scratch[2 + slot][g] = _dot_nt(qst, kx_ref[...])

    def softmax(slot):
        for g in range(len(masks_b)):
            _softmax_stage(scratch, slot, g, sink_ref[g * chain_rows:(g + 1) * chain_rows, :])

    def pv(nb, slot):
        q0, k0 = geometry(nb)
        outs = []
        for g in range(len(masks_b)):
            o = _pv_stage(scratch, slot, g, v_ref[pl.ds(k0, band), :], vx_ref[...])
            outs.append(_unstack_heads(o, SWA_BLOCK, g * hpc))
        o_ref[pl.ds(q0, SWA_BLOCK), :] = jnp.concatenate(outs, axis=1).astype(BF16)

    return scores, softmax, pv


ATTENTION_STEPS = GRID_H // NAT_ROWS_PER_STEP
assert ATTENTION_STEPS == SEQ // SWA_BLOCK


def _attention_kernel(*refs, ctx_out):
    refs = list(refs)
    take = lambda n: [refs.pop(0) for _ in range(n)]
    nq, nk, nv, nkx, nvx, sq, sk, sv, skx, svx, bias_ref, sink_ref, csink_ref = take(13)
    cqn, cqs = take(2) if ctx_out else (None, None)
    o_nat, o_swa = take(2)
    oc_nat, oc_swa = take(2) if ctx_out else (None, None)
    (win_ref,), nat_scratch, swa_scratch = take(1), take(10), take(10)
    mixers = [_nat_stages(nq, nk, nv, nkx, nvx, bias_ref, o_nat, nat_scratch),
              _swa_stages(sq, sk, sv, skx, svx, sink_ref, win_ref, o_swa, swa_scratch)]

    def stage(i):
        return lambda *args: [mixer[i](*args) for mixer in mixers]

    _pipelined_loop(ATTENTION_STEPS, stage(0), stage(1), stage(2))
    if ctx_out:
        _ctx_dense([(cqn, nkx, nvx, None, oc_nat), (cqs, skx, svx, csink_ref, oc_swa)])


def _attention(nat_lat, nat_ctx, swa_lat, swa_ctx, nat_bias, sink_lat, sink_ctx, layer, ctx_out):
    batch = nat_lat[0].shape[0] // SEQ
    lat_blk = pl.BlockSpec((SEQ, GROUP_W), lambda b: (b, 0))
    ctx_blk = pl.BlockSpec((CTX_LEN, GROUP_W), lambda b: (b, 0))
    consts = (nat_bias, sink_lat, sink_ctx)
    in_specs = ([lat_blk] * 3 + [ctx_blk] * 2) * 2 + [_resident(c.shape[1:], (layer,)) for c in consts]
    args = list(nat_lat) + list(nat_ctx[1:]) + list(swa_lat) + list(swa_ctx[1:]) + list(consts)
    out_shape = [jax.ShapeDtypeStruct((batch * SEQ, GROUP_W), BF16)] * 2
    out_specs = [lat_blk] * 2
    if ctx_out:
        in_specs += [ctx_blk] * 2
        args += [nat_ctx[0], swa_ctx[0]]
        out_shape += [jax.ShapeDtypeStruct((batch * CTX_LEN, GROUP_W), BF16)] * 2
        out_specs += [ctx_blk] * 2
    swa_rows, swa_chains = SWA_HEADS_PER_CHAIN * SWA_BLOCK, N_HEADS // SWA_HEADS_PER_CHAIN
    scratch = ((pltpu.VMEM((3, swa_rows, 3 * SWA_BLOCK), F32),)
               + _pipeline_scratch(NAT_ROWS_PER_STEP, N_HEADS * GRID_W, NAT_ROWS * GRID_W)
               + _pipeline_scratch(swa_chains, swa_rows, 3 * SWA_BLOCK))
    outs = pl.pallas_call(
        functools.partial(_attention_kernel, ctx_out=ctx_out), grid=(batch,), in_specs=in_specs,
        out_specs=out_specs, out_shape=out_shape, scratch_shapes=list(scratch),
        compiler_params=_params(1), name="attention",
    )(*args)
    return tuple(outs) if ctx_out else (outs[0], outs[1], None, None)


def _rope_tables():
    t = np.arange(SEQ)
    pos = np.stack([t // GRID_W, t % GRID_W], axis=1).astype(np.float32)
    half = HEAD_DIM // 4
    inv = ROPE_BASE ** (-jnp.arange(half, dtype=F32) / half)
    ang = jnp.asarray(pos)[:, :, None] * inv[None, None, :]
    cos = jnp.cos(ang)
    sin = jnp.sin(ang)
    cos64 = jnp.concatenate([cos, cos], axis=-1).reshape(SEQ, HEAD_DIM)
    sin64 = jnp.concatenate([-sin, sin], axis=-1).reshape(SEQ, HEAD_DIM)
    return jnp.tile(cos64, (1, V7X_LANES // HEAD_DIM)), jnp.tile(sin64, (1, V7X_LANES // HEAD_DIM))


def kernel(x, c, ctx, c_ctx, w_ada, b_ada, norm_g, ffn_w1, ffn_w3, ffn_w2, w_in, conv_w, conv_b, conv_ln_g,
           conv_ln_b, nat_rel_bias, sink_logits, w_out):
    bsz, seq, d = x.shape
    depth = w_ada.shape[0]
    assert (seq, d, ctx.shape[1]) == (SEQ, D_MODEL, CTX_LEN) and bsz < MOD_ROWS
    xl = x.reshape(bsz * seq, d)
    xc = ctx.reshape(bsz * CTX_LEN, d)

    cv = jnp.zeros((MOD_ROWS, d), F32).at[:bsz].set(c).at[bsz].set(c_ctx)
    mod = _ada_mod(cv, w_ada, b_ada).reshape(depth, MOD_ROWS, N_MOD, d)
    rope_tabs = _rope_tables()
    dft_lat = _dft_tables(SEQ)
    dft_ctx = _dft_tables(CTX_LEN)
    w1, w3, w2, wx, wo = (w.astype(BF16) for w in (ffn_w1, ffn_w3, ffn_w2, w_in, w_out))
    conv_p = jnp.stack([conv_b, conv_ln_g, conv_ln_b], axis=1)
    nat_bias = _nat_bias_table(nat_rel_bias)
    sink_lat = jnp.repeat(sink_logits * LOG2_E, SWA_BLOCK, axis=1)[..., None]
    sink_ctx = jnp.repeat(sink_logits * LOG2_E, CTX_LEN, axis=1)[..., None]

    for l in range(depth):
        last = l == depth - 1
        xl1, ua, ub, qn, kn, vn, qs, ks, vs = _front(xl, mod, 0, SEQ, l, norm_g, w1, w3, w2, wx, rope_tabs)
        xc1, ca, cb, cqn, ckn, cvn, cqs, cks, cvs = _front(xc, mod, bsz, bsz * CTX_LEN, l, norm_g, w1, w3, w2,
                                                           wx, None)
        y_a = _conv(ua, conv_w, conv_p, l, SEQ)
        y_b = _fourier(ub, *dft_lat, SEQ)
        y_c, y_d, yc_c, yc_d = _attention((qn, kn, vn), (cqn, ckn, cvn), (qs, ks, vs), (cqs, cks, cvs),
                                          nat_bias, sink_lat, sink_ctx, l, not last)
        xl = _back(xl1, (y_a, y_b, y_c, y_d), mod, 0, SEQ, l, norm_g, wo, w1, w3, w2)
        if not last:
            yc_a = _conv(ca, conv_w, conv_p, l, CTX_LEN)
            yc_b = _fourier(cb, *dft_ctx, CTX_LEN)
            xc = _back(xc1, (yc_a, yc_b, yc_c, yc_d), mod, bsz, bsz * CTX_LEN, l, norm_g, wo, w1, w3, w2)
    return xl.reshape(bsz, seq, d)
```

```python
import functools

import numpy as np
import jax
import jax.numpy as jnp
from jax import lax
from jax.experimental import pallas as pl
from jax.experimental.pallas import tpu as pltpu

F32 = jnp.float32
BF16 = jnp.bfloat16

D_MODEL = 1024
SEQ = 2048
CTX_LEN = 256
GRID_W = 64
GRID_H = SEQ // GRID_W
GROUP_W = 256
HEAD_DIM = 64
N_HEADS = 4
CONV_K = 31
NAT_ROWS = 8
NAT_COLS = 16
SWA_KV_HEADS = 2
SWA_WINDOW = 128
SWA_BLOCK = 128
ROPE_BASE = 10000.0
FFN_DIM = 2816
MACARON_W = 0.5
N_MOD = 9
MOD_ROWS = 24
EPS = 1e-6
NEG = -1e30
LOG2_E = float(np.log2(np.e))
Q_SCALE = HEAD_DIM ** -0.5 * LOG2_E

V7X_LANES = 128
V7X_SUBLANES = 8
V7X_MXU_DIM = 256
V7X_VMEM_LIMIT = 56 * 1024 * 1024
FRONT_ROW_TILE = 512
BACK_ROW_TILE = 1024
SUB_TILE_ROWS = 256
FFN_CHUNK_BOUNDS = (0, 6 * V7X_MXU_DIM, FFN_DIM)
CONV_HALO = 2 * V7X_SUBLANES

COL_UA = 0
COL_UB = 512
COL_QN, COL_KN, COL_VN = 768, 1024, 1280
COL_QS, COL_KS, COL_VS = 1536, 1792, 1920
IN_DIM = 2048


def _resident(shape, lead=()):
    nd = len(shape)
    return pl.BlockSpec((None,) * len(lead) + tuple(shape), lambda *_: tuple(lead) + (0,) * nd,
                        pipeline_mode=pl.Buffered(1))


def _params(n_axes=1):
    return pltpu.CompilerParams(dimension_semantics=("arbitrary",) * n_axes, vmem_limit_bytes=V7X_VMEM_LIMIT)


def _rms(x, g):
    return x * lax.rsqrt(jnp.mean(x * x, axis=-1, keepdims=True) + EPS) * g


def _sigmoid(x):
    return 1.0 / (1.0 + jnp.exp(-x))


def _sub_tiles(rows):
    assert rows % SUB_TILE_ROWS == 0
    return [slice(r0, r0 + SUB_TILE_ROWS) for r0 in range(0, rows, SUB_TILE_ROWS)]


def _ffn(xs, shift, scale, gate, g_pre, g_post, w1_ref, w3_ref, w2_ref):
    hs = [(_rms(x, g_pre) * (1.0 + scale) + shift).astype(BF16) for x in xs]
    ys = [None] * len(xs)
    for lo, hi in zip(FFN_CHUNK_BOUNDS[:-1], FFN_CHUNK_BOUNDS[1:]):
        ts = []
        for h in hs:
            a = jnp.dot(h, w1_ref[:, lo:hi], preferred_element_type=F32)
            b = jnp.dot(h, w3_ref[:, lo:hi], preferred_element_type=F32)
            ts.append((a * _sigmoid(a) * b).astype(BF16))
        for i, t in enumerate(ts):
            yc = jnp.dot(t, w2_ref[lo:hi, :], preferred_element_type=F32)
            ys[i] = yc if ys[i] is None else ys[i] + yc
    return [x + (MACARON_W * gate) * _rms(y, g_post) for x, y in zip(xs, ys)]


def _ada_kernel(cv_ref, w_ref, b_ref, o_ref):
    c = cv_ref[...]
    sc = (c * _sigmoid(c)).astype(BF16)
    o_ref[...] = jnp.dot(sc, w_ref[...].astype(BF16), preferred_element_type=F32) + b_ref[...]


def _ada_mod(cv, w_ada, b_ada):
    depth, d, n = w_ada.shape
    rows = cv.shape[0]
    tn = 9 * V7X_LANES
    return pl.pallas_call(
        _ada_kernel,
        grid=(depth, n // tn),
        in_specs=[pl.BlockSpec((rows, d), lambda l, j: (0, 0)),
                  pl.BlockSpec((None, d, tn), lambda l, j: (l, 0, j)),
                  pl.BlockSpec((None, 1, tn), lambda l, j: (l, 0, j))],
        out_specs=pl.BlockSpec((None, rows, tn), lambda l, j: (l, 0, j)),
        out_shape=jax.ShapeDtypeStruct((depth, rows, n), F32),
        compiler_params=_params(2),
        name="ada_mod",
    )(cv, w_ada, b_ada.reshape(depth, 1, n))


def _lane_index(shape):
    return lax.broadcasted_iota(jnp.int32, shape, 1)


def _rotate_half_partner(t):
    quarter = HEAD_DIM // 4
    up = pltpu.roll(t, V7X_LANES - quarter, axis=1)
    down = pltpu.roll(t, quarter, axis=1)
    return jnp.where(_lane_index(t.shape) % (2 * quarter) < quarter, up, down)


def _per_query_head(t):
    swapped = pltpu.roll(t, HEAD_DIM, axis=1)
    first = _lane_index(t.shape) < HEAD_DIM
    return jnp.concatenate([jnp.where(first, t, swapped), jnp.where(first, swapped, t)], axis=1)


def _front_kernel(*refs, rope):
    if rope:
        (x_ref, mod_ref, g_ref, w1_ref, w3_ref, w2_ref, wx_ref, cos_ref, sin_ref,
         x1_ref, ua_ref, ub_ref, qn_ref, kn_ref, vn_ref, qs_ref, ks_ref, vs_ref) = refs
    else:
        (x_ref, mod_ref, g_ref, w1_ref, w3_ref, w2_ref, wx_ref,
         x1_ref, ua_ref, ub_ref, qn_ref, kn_ref, vn_ref, qs_ref, ks_ref, vs_ref) = refs
    tiles = _sub_tiles(x_ref.shape[0])
    x1s = _ffn([x_ref[rows, :] for rows in tiles], mod_ref[0:1, :], mod_ref[1:2, :], mod_ref[2:3, :],
               g_ref[0:1, :], g_ref[1:2, :], w1_ref, w3_ref, w2_ref)
    hs = []
    for rows, x1 in zip(tiles, x1s):
        x1_ref[rows, :] = x1
        hs.append((_rms(x1, g_ref[2:3, :]) * (1.0 + mod_ref[4:5, :]) + mod_ref[3:4, :]).astype(BF16))
    us = [jnp.dot(h, wx_ref[...], preferred_element_type=F32) for h in hs]
    for rows, u in zip(tiles, us):
        ua_ref[rows, :] = u[:, COL_UA:COL_UA + 512]
        ub_ref[rows, :] = u[:, COL_UB:COL_UB + 256].astype(BF16)
        qn_ref[rows, :] = (u[:, COL_QN:COL_QN + 256] * Q_SCALE).astype(BF16)
        kn_ref[rows, :] = u[:, COL_KN:COL_KN + 256].astype(BF16)
        vn_ref[rows, :] = u[:, COL_VN:COL_VN + 256].astype(BF16)
        q_tiles = [u[:, COL_QS + i * V7X_LANES:COL_QS + (i + 1) * V7X_LANES] for i in range(2)]
        ks = u[:, COL_KS:COL_KS + V7X_LANES]
        if rope:
            cos, sin = cos_ref[rows, :], sin_ref[rows, :]
            q_tiles = [t * cos + _rotate_half_partner(t) * sin for t in q_tiles]
            ks = ks * cos + _rotate_half_partner(ks) * sin
        qs_ref[rows, :] = (jnp.concatenate(q_tiles, axis=1) * Q_SCALE).astype(BF16)
        ks_ref[rows, :] = _per_query_head(ks).astype(BF16)
        vs_ref[rows, :] = _per_query_head(u[:, COL_VS:COL_VS + V7X_LANES]).astype(BF16)


def _mod_spec(layer, row0, tiles_per_batch):
    return pl.BlockSpec((None, None, N_MOD, D_MODEL), lambda i: (layer, row0 + i // tiles_per_batch, 0, 0))


def _front(x, mod, mod_row0, rows_per_batch, layer, g, w1, w3, w2, wx, rope_tabs):
    rows, d = x.shape
    tm = FRONT_ROW_TILE
    tiles_per_batch = rows_per_batch // tm
    rope = rope_tabs is not None
    row_tile = lambda w: pl.BlockSpec((tm, w), lambda i: (i, 0))
    in_specs = [row_tile(d), _mod_spec(layer, mod_row0, tiles_per_batch),
                _resident(g.shape[1:], (layer,)), _resident(w1.shape[2:], (layer, 0)),
                _resident(w3.shape[2:], (layer, 0)), _resident(w2.shape[2:], (layer, 0)),
                _resident(wx.shape[1:], (layer,))]
    args = [x, mod, g, w1, w3, w2, wx]
    if rope:
        tab = pl.BlockSpec((tm, V7X_LANES), lambda i: (i % tiles_per_batch, 0))
        in_specs += [tab, tab]
        args += list(rope_tabs)
    out_shape = ([jax.ShapeDtypeStruct((rows, d), F32), jax.ShapeDtypeStruct((rows, 512), F32)]
                 + [jax.ShapeDtypeStruct((rows, 256), BF16)] * 7)
    out_specs = [row_tile(d), row_tile(512)] + [row_tile(256)] * 7
    return pl.pallas_call(
        functools.partial(_front_kernel, rope=rope),
        grid=(rows // tm,), in_specs=in_specs, out_specs=out_specs, out_shape=out_shape,
        compiler_params=_params(1), name="front_rope" if rope else "front_ctx",
    )(*args)


def _back_kernel(x_ref, ya_ref, yb_ref, yc_ref, yd_ref, mod_ref, g_ref, wo_ref, w1_ref, w3_ref, w2_ref, o_ref):
    tiles = _sub_tiles(x_ref.shape[0])
    x2s = []
    for rows in tiles:
        y = None
        for j, y_ref in enumerate((ya_ref, yb_ref, yc_ref, yd_ref)):
            yj = jnp.dot(y_ref[rows, :], wo_ref[j * GROUP_W:(j + 1) * GROUP_W, :], preferred_element_type=F32)
            y = yj if y is None else y + yj
        x2s.append(x_ref[rows, :] + mod_ref[5:6, :] * _rms(y, g_ref[3:4, :]))
    outs = _ffn(x2s, mod_ref[6:7, :], mod_ref[7:8, :], mod_ref[8:9, :],
                g_ref[4:5, :], g_ref[5:6, :], w1_ref, w3_ref, w2_ref)
    for rows, out in zip(tiles, outs):
        o_ref[rows, :] = out


def _back(x, ys, mod, mod_row0, rows_per_batch, layer, g, wo, w1, w3, w2):
    rows, d = x.shape
    tm = BACK_ROW_TILE
    tiles_per_batch = rows_per_batch // tm
    row_tile = lambda w: pl.BlockSpec((tm, w), lambda i: (i, 0))
    in_specs = ([row_tile(d)] + [row_tile(GROUP_W)] * 4
                + [_mod_spec(layer, mod_row0, tiles_per_batch),
                   _resident(g.shape[1:], (layer,)), _resident(wo.shape[1:], (layer,)),
                   _resident(w1.shape[2:], (layer, 1)), _resident(w3.shape[2:], (layer, 1)),
                   _resident(w2.shape[2:], (layer, 1))])
    return pl.pallas_call(
        _back_kernel, grid=(rows // tm,), in_specs=in_specs,
        out_specs=row_tile(d), out_shape=jax.ShapeDtypeStruct((rows, d), F32),
        compiler_params=_params(1), name="back",
    )(x, *ys, mod, g, wo, w1, w3, w2)


CONV_SUB_ROWS = 128
MIX_PIECE_ROWS = 256


def _conv_rows(pad_ref, w_ref, p_ref, row0, rows):
    bias, ln_g, ln_b = p_ref[0:1, :], p_ref[1:2, :], p_ref[2:3, :]
    lead = CONV_HALO - CONV_K // 2
    zrows = rows + V7X_SUBLANES
    acc = jnp.zeros((rows, GROUP_W), F32) + bias
    for b in range(V7X_SUBLANES):
        z = None
        for a in range((lead + CONV_K - 1) // V7X_SUBLANES + 1):
            k = V7X_SUBLANES * a + b - lead
            if 0 <= k < CONV_K:
                r0 = row0 + V7X_SUBLANES * a
                term = pad_ref[r0:r0 + zrows, :] * w_ref[k:k + 1, :]
                z = term if z is None else z + term
        acc = acc + z[b:b + rows, :]
    mu = jnp.mean(acc, axis=-1, keepdims=True)
    cen = acc - mu
    var = jnp.mean(cen * cen, axis=-1, keepdims=True)
    yn = cen * lax.rsqrt(var + EPS) * ln_g + ln_b
    return (yn * _sigmoid(yn)).astype(BF16)


def _conv_fourier_kernel(ua_ref, ub_ref, w_ref, p_ref, wc_ref, m_ref, ya_ref, yb_ref, pad_ref, *, n, scale):
    u = ua_ref[...]
    halo = jnp.zeros((CONV_HALO, GROUP_W), F32)
    pad_ref[0:CONV_HALO, :] = halo
    pad_ref[CONV_HALO:CONV_HALO + n, :] = u[:, :GROUP_W] * _sigmoid(u[:, GROUP_W:])
    pad_ref[CONV_HALO + n:, :] = halo
    t = jnp.dot(ub_ref[...], wc_ref[...], preferred_element_type=F32)
    zz = jnp.concatenate([t[:, :GROUP_W], t[:, GROUP_W:]], axis=0).astype(BF16)
    for r0 in range(0, n, MIX_PIECE_ROWS):
        for s0 in range(r0, r0 + MIX_PIECE_ROWS, CONV_SUB_ROWS):
            ya_ref[s0:s0 + CONV_SUB_ROWS, :] = _conv_rows(pad_ref, w_ref, p_ref, s0, CONV_SUB_ROWS)
        y = jnp.dot(m_ref[r0:r0 + MIX_PIECE_ROWS, :], zz, preferred_element_type=F32)
        yb_ref[r0:r0 + MIX_PIECE_ROWS, :] = (y * scale).astype(BF16)


def _unit_circle(n, period):
    i = jnp.arange(n, dtype=jnp.int32)[:, None]
    j = jnp.arange(period, dtype=jnp.int32)[None, :]
    ang = (2.0 * np.pi / period) * ((i * j) % period).astype(F32)
    return jnp.cos(ang), jnp.sin(ang)


def _dft_tables(n):
    hi = max(n // GRID_W, 1)
    lo = n // hi
    k = jnp.arange(n, dtype=jnp.int32)[None, :]
    n1 = jnp.arange(hi, dtype=jnp.int32)[:, None]
    n0 = jnp.arange(lo, dtype=jnp.int32)[:, None]
    a1 = (2.0 * np.pi / hi) * ((n1 * k) % hi).astype(F32)
    a0 = (2.0 * np.pi / n) * ((n0 * k) % n).astype(F32)
    c1, s1, c0, s0 = jnp.cos(a1)[:, None], jnp.sin(a1)[:, None], jnp.cos(a0)[None], jnp.sin(a0)[None]
    cos = (c1 * c0 - s1 * s0).reshape(n, n)
    sin = (s1 * c0 + c1 * s0).reshape(n, n)
    m = jnp.concatenate([cos, -sin], axis=1).astype(BF16)
    cc, sc = _unit_circle(HEAD_DIM, HEAD_DIM)
    eye = jnp.eye(GROUP_W // HEAD_DIM, dtype=F32)
    wc = jnp.concatenate([jnp.kron(eye, cc), jnp.kron(eye, sc)], axis=1).astype(BF16)
    return wc, m


def _conv_fourier(ua, ub, conv_w, conv_p, wc, m, layer, rows_per_batch):
    rows = ua.shape[0]
    n = rows_per_batch
    assert n % MIX_PIECE_ROWS == 0
    blk = lambda w: pl.BlockSpec((n, w), lambda b: (b, 0))
    return pl.pallas_call(
        functools.partial(_conv_fourier_kernel, n=n, scale=float((n * HEAD_DIM) ** -0.5)),
        grid=(rows // n,),
        in_specs=[blk(2 * GROUP_W), blk(GROUP_W), _resident(conv_w.shape[1:], (layer,)),
                  _resident(conv_p.shape[1:], (layer,)), _resident(wc.shape), _resident(m.shape)],
        out_specs=[blk(GROUP_W), blk(GROUP_W)],
        out_shape=[jax.ShapeDtypeStruct((rows, GROUP_W), BF16)] * 2,
        scratch_shapes=[pltpu.VMEM((n + 2 * CONV_HALO, GROUP_W), F32)],
        compiler_params=_params(1), name="conv_fourier",
    )(ua, ub, conv_w, conv_p, wc, m)


def _head_mask(rows_per_head, heads=N_HEADS, head0=0):
    shape = (heads * rows_per_head, GROUP_W)
    row_h = lax.broadcasted_iota(jnp.int32, shape, 0) // rows_per_head + head0
    lane_h = lax.broadcasted_iota(jnp.int32, shape, 1) // HEAD_DIM
    return (row_h == lane_h).astype(F32)


def _stack_heads(q, mask_bf16):
    heads = mask_bf16.shape[0] // q.shape[0]
    return jnp.concatenate([q] * heads, axis=0) * mask_bf16


def _unstack_heads(o, rows_per_head, head0=0):
    heads = o.shape[0] // rows_per_head
    assert heads % 2 == 0 and head0 % 2 == 0
    first = _lane_index((rows_per_head, V7X_LANES)) < HEAD_DIM
    tiles = []
    for pair in range(heads // 2):
        lanes = slice((head0 // 2 + pair) * V7X_LANES, (head0 // 2 + pair + 1) * V7X_LANES)
        lo = o[2 * pair * rows_per_head:(2 * pair + 1) * rows_per_head, lanes]
        hi = o[(2 * pair + 1) * rows_per_head:(2 * pair + 2) * rows_per_head, lanes]
        tiles.append(jnp.where(first, lo, hi))
    return tiles[0] if len(tiles) == 1 else jnp.concatenate(tiles, axis=1)


def _dot_nt(a, b):
    return lax.dot_general(a, b, (((1,), (1,)), ((), ())), preferred_element_type=F32)


def _tree(op, xs):
    while len(xs) > 1:
        xs = [op(xs[i], xs[i + 1]) if i + 1 < len(xs) else xs[i] for i in range(0, len(xs), 2)]
    return xs[0]


def _lane_tiles(*arrays):
    return [a[:, i:i + V7X_LANES] for a in arrays for i in range(0, a.shape[1], V7X_LANES)]


def _ctx_dense(jobs):
    hpc = 2
    chains = [(job, g) for job in jobs for g in range(N_HEADS // hpc)]
    scores = [_dot_nt(_stack_heads(cq_ref[...], _head_mask(CTX_LEN, hpc, g * hpc).astype(BF16)), kx_ref[...])
              for (cq_ref, kx_ref, _, _, _), g in chains]
    probs = []
    for ((_, _, _, sink_ref, _), g), s in zip(chains, scores):
        m = jnp.max(_tree(jnp.maximum, _lane_tiles(s)), axis=-1, keepdims=True)
        sink = None if sink_ref is None else sink_ref[g * hpc * CTX_LEN:(g + 1) * hpc * CTX_LEN, :]
        if sink is not None:
            m = jnp.maximum(m, sink)
        p = jnp.exp2(s - m)
        den = jnp.sum(_tree(jnp.add, _lane_tiles(p)), axis=-1, keepdims=True)
        if sink is not None:
            den = den + jnp.exp2(sink - m)
        probs.append((p.astype(BF16), den))
    outs = [_unstack_heads(jnp.dot(p, vx_ref[...], preferred_element_type=F32) / den, CTX_LEN, g * hpc)
            for ((_, _, vx_ref, _, _), g), (p, den) in zip(chains, probs)]
    for i, (_, _, _, _, oc_ref) in enumerate(jobs):
        oc_ref[...] = jnp.concatenate(outs[i * (N_HEADS // hpc):(i + 1) * (N_HEADS // hpc)], axis=1).astype(BF16)


NAT_ROWS_PER_STEP = 2


def _pipelined_loop(n_steps, scores_fn, softmax_fn, pv_fn):
    scores_fn(0, 0)

    def pair(it, carry):
        for slot in (0, 1):
            step = 2 * it + slot
            scores_fn(jnp.minimum(step + 1, n_steps - 1), 1 - slot)
            softmax_fn(slot)
            pv_fn(step, slot)
        return carry

    lax.fori_loop(0, n_steps // 2, pair, 0)


def _pipeline_scratch(chains, rows, n_loc):
    kinds = [((chains, rows, n_loc), F32), ((chains, rows, CTX_LEN), F32),
             ((chains, rows, n_loc), BF16), ((chains, rows, CTX_LEN), BF16), ((chains, rows, 1), F32)]
    return tuple(pltpu.VMEM(shape, dtype) for shape, dtype in kinds for _ in range(2))


def _softmax_stage(scratch, slot, chain, sink):
    sl, sc, pl_, pc, dn = (scratch[2 * i + slot] for i in range(5))
    s, c = sl[chain], sc[chain]
    m = jnp.max(_tree(jnp.maximum, _lane_tiles(s, c)), axis=-1, keepdims=True)
    if sink is not None:
        m = jnp.maximum(m, sink)
    p = jnp.exp2(s - m)
    q = jnp.exp2(c - m)
    den = jnp.sum(_tree(jnp.add, _lane_tiles(p, q)), axis=-1, keepdims=True)
    if sink is not None:
        den = den + jnp.exp2(sink - m)
    pl_[chain] = p.astype(BF16)
    pc[chain] = q.astype(BF16)
    dn[chain] = den


def _pv_stage(scratch, slot, chain, vb, vx):
    pl_, pc, dn = (scratch[2 * i + slot] for i in (2, 3, 4))
    o = (jnp.dot(pl_[chain], vb, preferred_element_type=F32) + jnp.dot(pc[chain], vx, preferred_element_type=F32))
    return o / dn[chain]


def _nat_stages(q_ref, k_ref, v_ref, kx_ref, vx_ref, bias_ref, o_ref, scratch):
    mask_b = _head_mask(GRID_W).astype(BF16)
    band = NAT_ROWS * GRID_W
    rps = NAT_ROWS_PER_STEP

    def geometry(r):
        rs = jnp.clip(r - NAT_ROWS // 2, 0, GRID_H - NAT_ROWS)
        return pl.multiple_of(r * GRID_W, GRID_W), pl.multiple_of(rs * GRID_W, GRID_W), rs - r + NAT_ROWS - 1

    def scores(step, slot):
        for j in range(rps):
            q0, k0, d0 = geometry(step * rps + j)
            qst = _stack_heads(q_ref[pl.ds(q0, GRID_W), :], mask_b)
            bias = jnp.concatenate([bias_ref[d0 + 2 * a] for a in range(NAT_ROWS // 2)], axis=1)
            scratch[slot][j] = _dot_nt(qst, k_ref[pl.ds(k0, band), :]) + bias
            scratch[2 + slot][j] = _dot_nt(qst, kx_ref[...])

    def softmax(slot):
        for j in range(rps):
            _softmax_stage(scratch, slot, j, None)

    def pv(step, slot):
        for j in range(rps):
            q0, k0, _ = geometry(step * rps + j)
            o = _pv_stage(scratch, slot, j, v_ref[pl.ds(k0, band), :], vx_ref[...])
            o_ref[pl.ds(q0, GRID_W), :] = _unstack_heads(o, GRID_W).astype(BF16)

    return scores, softmax, pv


def _nat_bias_table(rel_bias):
    depth = rel_bias.shape[0]
    w = GRID_W
    n_dr = 2 * NAT_ROWS - 1
    lo = w - NAT_COLS
    g = jnp.pad(rel_bias.astype(F32) * LOG2_E, ((0, 0),) * 3 + ((lo, 2 * w - lo - (2 * NAT_COLS - 1)),))
    flat = jnp.tile(g, (1, 1, 1, w))[..., :w * (2 * w - 1)]
    t = flat.reshape(depth, N_HEADS, n_dr, w, 2 * w - 1)[..., w - 1:]
    qc = np.arange(w)[:, None]
    kc = np.arange(w)[None, :]
    ws = np.clip(qc - NAT_COLS // 2, 0, w - NAT_COLS)
    col_ok = (kc >= ws) & (kc < ws + NAT_COLS)
    t = jnp.where(col_ok, t, NEG).transpose(0, 2, 1, 3, 4).reshape(depth, n_dr, N_HEADS * w, w)
    return jnp.concatenate([t[:, :-1], t[:, 1:]], axis=-1)


SWA_HEADS_PER_CHAIN = 2


def _swa_stages(q_ref, k_ref, v_ref, kx_ref, vx_ref, sink_ref, win_ref, o_ref, scratch):
    hpc = SWA_HEADS_PER_CHAIN
    chain_rows = hpc * SWA_BLOCK
    band = 3 * SWA_BLOCK
    n_blocks = SEQ // SWA_BLOCK

    @pl.when(pl.program_id(0) == 0)
    def _():
        shape = (chain_rows, band)
        rel = (lax.broadcasted_iota(jnp.int32, shape, 1)
               - lax.broadcasted_iota(jnp.int32, shape, 0) % SWA_BLOCK)
        for i, shift in enumerate((0, -SWA_BLOCK, -2 * SWA_BLOCK)):
            win_ref[i] = jnp.where(jnp.abs(rel + shift) <= SWA_WINDOW, 0.0, NEG)

    masks_b = [_head_mask(SWA_BLOCK, hpc, g * hpc).astype(BF16) for g in range(N_HEADS // hpc)]

    def geometry(nb):
        q0 = pl.multiple_of(nb * SWA_BLOCK, SWA_BLOCK)
        return q0, pl.multiple_of(jnp.clip(q0 - SWA_BLOCK, 0, SEQ - band), SWA_BLOCK)

    def scores(nb, slot):
        q0, k0 = geometry(nb)
        placement = jnp.where(nb == 0, 0, jnp.where(nb == n_blocks - 1, 2, 1))
        q = q_ref[pl.ds(q0, SWA_BLOCK), :]
        for g in range(len(masks_b)):
            qst = _stack_heads(q, masks_b[g])
            scratch[slot][g] = _dot_nt(qst, k_ref[pl.ds(k0, band), :]) + win_ref[placement]
            scratch[2 + slot][g] = _dot_nt(qst, kx_ref[...])

    def softmax(slot):
        for g in range(len(masks_b)):
            _softmax_stage(scratch, slot, g, sink_ref[g * chain_rows:(g + 1) * chain_rows, :])

    def pv(nb, slot):
        q0, k0 = geometry(nb)
        outs = []
        for g in range(len(masks_b)):
            o = _pv_stage(scratch, slot, g, v_ref[pl.ds(k0, band), :], vx_ref[...])
            outs.append(_unstack_heads(o, SWA_BLOCK, g * hpc))
        o_ref[pl.ds(q0, SWA_BLOCK), :] = jnp.concatenate(outs, axis=1).astype(BF16)

    return scores, softmax, pv


ATTENTION_STEPS = GRID_H // NAT_ROWS_PER_STEP
assert ATTENTION_STEPS == SEQ // SWA_BLOCK


def _attention_kernel(*refs, ctx_out):
    refs = list(refs)
    take = lambda n: [refs.pop(0) for _ in range(n)]
    nq, nk, nv, nkx, nvx, sq, sk, sv, skx, svx, bias_ref, sink_ref, csink_ref = take(13)
    cqn, cqs = take(2) if ctx_out else (None, None)
    o_nat, o_swa = take(2)
    oc_nat, oc_swa = take(2) if ctx_out else (None, None)
    (win_ref,), nat_scratch, swa_scratch = take(1), take(10), take(10)
    mixers = [_nat_stages(nq, nk, nv, nkx, nvx, bias_ref, o_nat, nat_scratch),
              _swa_stages(sq, sk, sv, skx, svx, sink_ref, win_ref, o_swa, swa_scratch)]

    def stage(i):
        return lambda *args: [mixer[i](*args) for mixer in mixers]

    _pipelined_loop(ATTENTION_STEPS, stage(0), stage(1), stage(2))
    if ctx_out:
        _ctx_dense([(cqn, nkx, nvx, None, oc_nat), (cqs, skx, svx, csink_ref, oc_swa)])


def _attention(nat_lat, nat_ctx, swa_lat, swa_ctx, nat_bias, sink_lat, sink_ctx, layer, ctx_out):
    batch = nat_lat[0].shape[0] // SEQ
    lat_blk = pl.BlockSpec((SEQ, GROUP_W), lambda b: (b, 0))
    ctx_blk = pl.BlockSpec((CTX_LEN, GROUP_W), lambda b: (b, 0))
    consts = (nat_bias, sink_lat, sink_ctx)
    in_specs = ([lat_blk] * 3 + [ctx_blk] * 2) * 2 + [_resident(c.shape[1:], (layer,)) for c in consts]
    args = list(nat_lat) + list(nat_ctx[1:]) + list(swa_lat) + list(swa_ctx[1:]) + list(consts)
    out_shape = [jax.ShapeDtypeStruct((batch * SEQ, GROUP_W), BF16)] * 2
    out_specs = [lat_blk] * 2
    if ctx_out:
        in_specs += [ctx_blk] * 2
        args += [nat_ctx[0], swa_ctx[0]]
        out_shape += [jax.ShapeDtypeStruct((batch * CTX_LEN, GROUP_W), BF16)] * 2
        out_specs += [ctx_blk] * 2
    swa_rows, swa_chains = SWA_HEADS_PER_CHAIN * SWA_BLOCK, N_HEADS // SWA_HEADS_PER_CHAIN
    scratch = ((pltpu.VMEM((3, swa_rows, 3 * SWA_BLOCK), F32),)
               + _pipeline_scratch(NAT_ROWS_PER_STEP, N_HEADS * GRID_W, NAT_ROWS * GRID_W)
               + _pipeline_scratch(swa_chains, swa_rows, 3 * SWA_BLOCK))
    outs = pl.pallas_call(
        functools.partial(_attention_kernel, ctx_out=ctx_out), grid=(batch,), in_specs=in_specs,
        out_specs=out_specs, out_shape=out_shape, scratch_shapes=list(scratch),
        compiler_params=_params(1), name="attention",
    )(*args)
    return tuple(outs) if ctx_out else (outs[0], outs[1], None, None)


def _rope_tables():
    t = np.arange(SEQ)
    pos = np.stack([t // GRID_W, t % GRID_W], axis=1).astype(np.float32)
    half = HEAD_DIM // 4
    inv = ROPE_BASE ** (-jnp.arange(half, dtype=F32) / half)
    ang = jnp.asarray(pos)[:, :, None] * inv[None, None, :]
    cos = jnp.cos(ang)
    sin = jnp.sin(ang)
    cos64 = jnp.concatenate([cos, cos], axis=-1).reshape(SEQ, HEAD_DIM)
    sin64 = jnp.concatenate([-sin, sin], axis=-1).reshape(SEQ, HEAD_DIM)
    return jnp.tile(cos64, (1, V7X_LANES // HEAD_DIM)), jnp.tile(sin64, (1, V7X_LANES // HEAD_DIM))


def kernel(x, c, ctx, c_ctx, w_ada, b_ada, norm_g, ffn_w1, ffn_w3, ffn_w2, w_in, conv_w, conv_b, conv_ln_g,
           conv_ln_b, nat_rel_bias, sink_logits, w_out):
    bsz, seq, d = x.shape
    depth = w_ada.shape[0]
    assert (seq, d, ctx.shape[1]) == (SEQ, D_MODEL, CTX_LEN) and bsz < MOD_ROWS
    xl = x.reshape(bsz * seq, d)
    xc = ctx.reshape(bsz * CTX_LEN, d)

    cv = jnp.zeros((MOD_ROWS, d), F32).at[:bsz].set(c).at[bsz].set(c_ctx)
    mod = _ada_mod(cv, w_ada, b_ada).reshape(depth, MOD_ROWS, N_MOD, d)
    rope_tabs = _rope_tables()
    dft_lat = _dft_tables(SEQ)
    dft_ctx = _dft_tables(CTX_LEN)
    w1, w3, w2, wx, wo = (w.astype(BF16) for w in (ffn_w1, ffn_w3, ffn_w2, w_in, w_out))
    conv_p = jnp.stack([conv_b, conv_ln_g, conv_ln_b], axis=1)
    nat_bias = _nat_bias_table(nat_rel_bias)
    sink_lat = jnp.repeat(sink_logits * LOG2_E, SWA_BLOCK, axis=1)[..., None]
    sink_ctx = jnp.repeat(sink_logits * LOG2_E, CTX_LEN, axis=1)[..., None]

    for l in range(depth):
        last = l == depth - 1
        xl1, ua, ub, qn, kn, vn, qs, ks, vs = _front(xl, mod, 0, SEQ, l, norm_g, w1, w3, w2, wx, rope_tabs)
        xc1, ca, cb, cqn, ckn, cvn, cqs, cks, cvs = _front(xc, mod, bsz, bsz * CTX_LEN, l, norm_g, w1, w3, w2,
                                                           wx, None)
        y_a, y_b = _conv_fourier(ua, ub, conv_w, conv_p, *dft_lat, l, SEQ)
        y_c, y_d, yc_c, yc_d = _attention((qn, kn, vn), (cqn, ckn, cvn), (qs, ks, vs), (cqs, cks, cvs),
                                          nat_bias, sink_lat, sink_ctx, l, not last)
        xl = _back(xl1, (y_a, y_b, y_c, y_d), mod, 0, SEQ, l, norm_g, wo, w1, w3, w2)
        if not last:
            yc_a, yc_b = _conv_fourier(ca, cb, conv_w, conv_p, *dft_ctx, l, CTX_LEN)
            xc = _back(xc1, (yc_a, yc_b, yc_c, yc_d), mod, bsz, bsz * CTX_LEN, l, norm_g, wo, w1, w3, w2)
    return xl.reshape(bsz, seq, d)
```

```python
import functools

import numpy as np
import jax
import jax.numpy as jnp
from jax import lax
from jax.experimental import pallas as pl
from jax.experimental.pallas import tpu as pltpu

F32 = jnp.float32
BF16 = jnp.bfloat16

D_MODEL = 1024
SEQ = 2048
CTX_LEN = 256
GRID_W = 64
GRID_H = SEQ // GRID_W
GROUP_W = 256
HEAD_DIM = 64
N_HEADS = 4
CONV_K = 31
NAT_ROWS = 8
NAT_COLS = 16
SWA_KV_HEADS = 2
SWA_WINDOW = 128
SWA_BLOCK = 128
ROPE_BASE = 10000.0
FFN_DIM = 2816
MACARON_W = 0.5
N_MOD = 9
MOD_ROWS = 24
EPS = 1e-6
NEG = -1e30
LOG2_E = float(np.log2(np.e))
Q_SCALE = HEAD_DIM ** -0.5 * LOG2_E

V7X_LANES = 128
V7X_SUBLANES = 8
V7X_MXU_DIM = 256
V7X_VMEM_LIMIT = 56 * 1024 * 1024
FRONT_ROW_TILE = 512
BACK_ROW_TILE = 1024
SUB_TILE_ROWS = 256
FFN_CHUNK_BOUNDS = (0, 6 * V7X_MXU_DIM, FFN_DIM)
CONV_HALO = 2 * V7X_SUBLANES

COL_UA = 0
COL_UB = 512
COL_QN, COL_KN, COL_VN = 768, 1024, 1280
COL_QS, COL_KS, COL_VS = 1536, 1792, 1920
IN_DIM = 2048


def _resident(shape, lead=()):
    nd = len(shape)
    return pl.BlockSpec((None,) * len(lead) + tuple(shape), lambda *_: tuple(lead) + (0,) * nd,
                        pipeline_mode=pl.Buffered(1))


def _params(n_axes=1):
    return pltpu.CompilerParams(dimension_semantics=("arbitrary",) * n_axes, vmem_limit_bytes=V7X_VMEM_LIMIT)


def _rms(x, g):
    return x * lax.rsqrt(jnp.mean(x * x, axis=-1, keepdims=True) + EPS) * g


def _sigmoid(x):
    return 1.0 / (1.0 + jnp.exp(-x))


def _sub_tiles(rows):
    assert rows % SUB_TILE_ROWS == 0
    return [slice(r0, r0 + SUB_TILE_ROWS) for r0 in range(0, rows, SUB_TILE_ROWS)]


def _ffn(xs, shift, scale, gate, g_pre, g_post, w1_ref, w3_ref, w2_ref):
    hs = [(_rms(x, g_pre) * (1.0 + scale) + shift).astype(BF16) for x in xs]
    ys = [None] * len(xs)
    for lo, hi in zip(FFN_CHUNK_BOUNDS[:-1], FFN_CHUNK_BOUNDS[1:]):
        ts = []
        for h in hs:
            a = jnp.dot(h, w1_ref[:, lo:hi], preferred_element_type=F32)
            b = jnp.dot(h, w3_ref[:, lo:hi], preferred_element_type=F32)
            ts.append((a * _sigmoid(a) * b).astype(BF16))
        for i, t in enumerate(ts):
            yc = jnp.dot(t, w2_ref[lo:hi, :], preferred_element_type=F32)
            ys[i] = yc if ys[i] is None else ys[i] + yc
    return [x + (MACARON_W * gate) * _rms(y, g_post) for x, y in zip(xs, ys)]


def _ada_kernel(cv_ref, w_ref, b_ref, o_ref):
    c = cv_ref[...]
    sc = (c * _sigmoid(c)).astype(BF16)
    o_ref[...] = jnp.dot(sc, w_ref[...].astype(BF16), preferred_element_type=F32) + b_ref[...]


def _ada_mod(cv, w_ada, b_ada):
    depth, d, n = w_ada.shape
    rows = cv.shape[0]
    tn = 9 * V7X_LANES
    return pl.pallas_call(
        _ada_kernel,
        grid=(depth, n // tn),
        in_specs=[pl.BlockSpec((rows, d), lambda l, j: (0, 0)),
                  pl.BlockSpec((None, d, tn), lambda l, j: (l, 0, j)),
                  pl.BlockSpec((None, 1, tn), lambda l, j: (l, 0, j))],
        out_specs=pl.BlockSpec((None, rows, tn), lambda l, j: (l, 0, j)),
        out_shape=jax.ShapeDtypeStruct((depth, rows, n), F32),
        compiler_params=_params(2),
        name="ada_mod",
    )(cv, w_ada, b_ada.reshape(depth, 1, n))


def _lane_index(shape):
    return lax.broadcasted_iota(jnp.int32, shape, 1)


def _rotate_half_partner(t):
    quarter = HEAD_DIM // 4
    up = pltpu.roll(t, V7X_LANES - quarter, axis=1)
    down = pltpu.roll(t, quarter, axis=1)
    return jnp.where(_lane_index(t.shape) % (2 * quarter) < quarter, up, down)


def _per_query_head(t):
    swapped = pltpu.roll(t, HEAD_DIM, axis=1)
    first = _lane_index(t.shape) < HEAD_DIM
    return jnp.concatenate([jnp.where(first, t, swapped), jnp.where(first, swapped, t)], axis=1)


def _front_kernel(*refs, rope):
    if rope:
        (x_ref, mod_ref, g_ref, w1_ref, w3_ref, w2_ref, wx_ref, cos_ref, sin_ref,
         x1_ref, ua_ref, ub_ref, qn_ref, kn_ref, vn_ref, qs_ref, ks_ref, vs_ref) = refs
    else:
        (x_ref, mod_ref, g_ref, w1_ref, w3_ref, w2_ref, wx_ref,
         x1_ref, ua_ref, ub_ref, qn_ref, kn_ref, vn_ref, qs_ref, ks_ref, vs_ref) = refs
    tiles = _sub_tiles(x_ref.shape[0])
    x1s = _ffn([x_ref[rows, :] for rows in tiles], mod_ref[0:1, :], mod_ref[1:2, :], mod_ref[2:3, :],
               g_ref[0:1, :], g_ref[1:2, :], w1_ref, w3_ref, w2_ref)
    hs = []
    for rows, x1 in zip(tiles, x1s):
        x1_ref[rows, :] = x1
        hs.append((_rms(x1, g_ref[2:3, :]) * (1.0 + mod_ref[4:5, :]) + mod_ref[3:4, :]).astype(BF16))
    us = [jnp.dot(h, wx_ref[...], preferred_element_type=F32) for h in hs]
    for rows, u in zip(tiles, us):
        ua_ref[rows, :] = u[:, COL_UA:COL_UA + 512]
        ub_ref[rows, :] = u[:, COL_UB:COL_UB + 256].astype(BF16)
        qn_ref[rows, :] = (u[:, COL_QN:COL_QN + 256] * Q_SCALE).astype(BF16)
        kn_ref[rows, :] = u[:, COL_KN:COL_KN + 256].astype(BF16)
        vn_ref[rows, :] = u[:, COL_VN:COL_VN + 256].astype(BF16)
        q_tiles = [u[:, COL_QS + i * V7X_LANES:COL_QS + (i + 1) * V7X_LANES] for i in range(2)]
        ks = u[:, COL_KS:COL_KS + V7X_LANES]
        if rope:
            cos, sin = cos_ref[rows, :], sin_ref[rows, :]
            q_tiles = [t * cos + _rotate_half_partner(t) * sin for t in q_tiles]
            ks = ks * cos + _rotate_half_partner(ks) * sin
        qs_ref[rows, :] = (jnp.concatenate(q_tiles, axis=1) * Q_SCALE).astype(BF16)
        ks_ref[rows, :] = _per_query_head(ks).astype(BF16)
        vs_ref[rows, :] = _per_query_head(u[:, COL_VS:COL_VS + V7X_LANES]).astype(BF16)


def _mod_spec(layer, row0, tiles_per_batch):
    return pl.BlockSpec((None, None, N_MOD, D_MODEL), lambda i: (layer, row0 + i // tiles_per_batch, 0, 0))


def _front(x, mod, mod_row0, rows_per_batch, layer, g, w1, w3, w2, wx, rope_tabs):
    rows, d = x.shape
    tm = FRONT_ROW_TILE
    tiles_per_batch = rows_per_batch // tm
    rope = rope_tabs is not None
    row_tile = lambda w: pl.BlockSpec((tm, w), lambda i: (i, 0))
    in_specs = [row_tile(d), _mod_spec(layer, mod_row0, tiles_per_batch),
                _resident(g.shape[1:], (layer,)), _resident(w1.shape[2:], (layer, 0)),
                _resident(w3.shape[2:], (layer, 0)), _resident(w2.shape[2:], (layer, 0)),
                _resident(wx.shape[1:], (layer,))]
    args = [x, mod, g, w1, w3, w2, wx]
    if rope:
        tab = pl.BlockSpec((tm, V7X_LANES), lambda i: (i % tiles_per_batch, 0))
        in_specs += [tab, tab]
        args += list(rope_tabs)
    out_shape = ([jax.ShapeDtypeStruct((rows, d), F32), jax.ShapeDtypeStruct((rows, 512), F32)]
                 + [jax.ShapeDtypeStruct((rows, 256), BF16)] * 7)
    out_specs = [row_tile(d), row_tile(512)] + [row_tile(256)] * 7
    return pl.pallas_call(
        functools.partial(_front_kernel, rope=rope),
        grid=(rows // tm,), in_specs=in_specs, out_specs=out_specs, out_shape=out_shape,
        compiler_params=_params(1), name="front_rope" if rope else "front_ctx",
    )(*args)


def _back_kernel(x_ref, ya_ref, yb_ref, yc_ref, yd_ref, mod_ref, g_ref, wo_ref, w1_ref, w3_ref, w2_ref, o_ref):
    tiles = _sub_tiles(x_ref.shape[0])
    x2s = []
    for rows in tiles:
        y = None
        for j, y_ref in enumerate((ya_ref, yb_ref, yc_ref, yd_ref)):
            yj = jnp.dot(y_ref[rows, :], wo_ref[j * GROUP_W:(j + 1) * GROUP_W, :], preferred_element_type=F32)
            y = yj if y is None else y + yj
        x2s.append(x_ref[rows, :] + mod_ref[5:6, :] * _rms(y, g_ref[3:4, :]))
    outs = _ffn(x2s, mod_ref[6:7, :], mod_ref[7:8, :], mod_ref[8:9, :],
                g_ref[4:5, :], g_ref[5:6, :], w1_ref, w3_ref, w2_ref)
    for rows, out in zip(tiles, outs):
        o_ref[rows, :] = out


def _back(x, ys, mod, mod_row0, rows_per_batch, layer, g, wo, w1, w3, w2):
    rows, d = x.shape
    tm = BACK_ROW_TILE
    tiles_per_batch = rows_per_batch // tm
    row_tile = lambda w: pl.BlockSpec((tm, w), lambda i: (i, 0))
    in_specs = ([row_tile(d)] + [row_tile(GROUP_W)] * 4
                + [_mod_spec(layer, mod_row0, tiles_per_batch),
                   _resident(g.shape[1:], (layer,)), _resident(wo.shape[1:], (layer,)),
                   _resident(w1.shape[2:], (layer, 1)), _resident(w3.shape[2:], (layer, 1)),
                   _resident(w2.shape[2:], (layer, 1))])
    return pl.pallas_call(
        _back_kernel, grid=(rows // tm,), in_specs=in_specs,
        out_specs=row_tile(d), out_shape=jax.ShapeDtypeStruct((rows, d), F32),
        compiler_params=_params(1), name="back",
    )(x, *ys, mod, g, wo, w1, w3, w2)


CONV_SUB_ROWS = 128
MIX_PIECE_ROWS = 256


def _conv_rows(pad_ref, w_ref, p_ref, row0, rows):
    bias, ln_g, ln_b = p_ref[0:1, :], p_ref[1:2, :], p_ref[2:3, :]
    lead = CONV_HALO - CONV_K // 2
    zrows = rows + V7X_SUBLANES
    acc = jnp.zeros((rows, GROUP_W), F32) + bias
    for b in range(V7X_SUBLANES):
        z = None
        for a in range((lead + CONV_K - 1) // V7X_SUBLANES + 1):
            k = V7X_SUBLANES * a + b - lead
            if 0 <= k < CONV_K:
                r0 = row0 + V7X_SUBLANES * a
                term = pad_ref[r0:r0 + zrows, :] * w_ref[k:k + 1, :]
                z = term if z is None else z + term
        acc = acc + z[b:b + rows, :]
    mu = jnp.mean(acc, axis=-1, keepdims=True)
    cen = acc - mu
    var = jnp.mean(cen * cen, axis=-1, keepdims=True)
    yn = cen * lax.rsqrt(var + EPS) * ln_g + ln_b
    return (yn * _sigmoid(yn)).astype(BF16)


def _conv_fourier_kernel(ua_ref, ub_ref, w_ref, p_ref, wc_ref, m_ref, ya_ref, yb_ref, pad_ref, *, n, scale):
    u = ua_ref[...]
    halo = jnp.zeros((CONV_HALO, GROUP_W), F32)
    pad_ref[0:CONV_HALO, :] = halo
    pad_ref[CONV_HALO:CONV_HALO + n, :] = u[:, :GROUP_W] * _sigmoid(u[:, GROUP_W:])
    pad_ref[CONV_HALO + n:, :] = halo
    t = jnp.dot(ub_ref[...], wc_ref[...], preferred_element_type=F32)
    zz = jnp.concatenate([t[:, :GROUP_W], t[:, GROUP_W:]], axis=0).astype(BF16)
    for r0 in range(0, n, MIX_PIECE_ROWS):
        for s0 in range(r0, r0 + MIX_PIECE_ROWS, CONV_SUB_ROWS):
            ya_ref[s0:s0 + CONV_SUB_ROWS, :] = _conv_rows(pad_ref, w_ref, p_ref, s0, CONV_SUB_ROWS)
        y = jnp.dot(m_ref[r0:r0 + MIX_PIECE_ROWS, :], zz, preferred_element_type=F32)
        yb_ref[r0:r0 + MIX_PIECE_ROWS, :] = (y * scale).astype(BF16)


def _unit_circle(n, period):
    i = jnp.arange(n, dtype=jnp.int32)[:, None]
    j = jnp.arange(period, dtype=jnp.int32)[None, :]
    ang = (2.0 * np.pi / period) * ((i * j) % period).astype(F32)
    return jnp.cos(ang), jnp.sin(ang)


def _dft_tables(n):
    hi = max(n // GRID_W, 1)
    lo = n // hi
    k = jnp.arange(n, dtype=jnp.int32)[None, :]
    n1 = jnp.arange(hi, dtype=jnp.int32)[:, None]
    n0 = jnp.arange(lo, dtype=jnp.int32)[:, None]
    a1 = (2.0 * np.pi / hi) * ((n1 * k) % hi).astype(F32)
    a0 = (2.0 * np.pi / n) * ((n0 * k) % n).astype(F32)
    c1, s1, c0, s0 = jnp.cos(a1)[:, None], jnp.sin(a1)[:, None], jnp.cos(a0)[None], jnp.sin(a0)[None]
    cos = (c1 * c0 - s1 * s0).reshape(n, n)
    sin = (s1 * c0 + c1 * s0).reshape(n, n)
    m = jnp.concatenate([cos, -sin], axis=1).astype(BF16)
    cc, sc = _unit_circle(HEAD_DIM, HEAD_DIM)
    eye = jnp.eye(GROUP_W // HEAD_DIM, dtype=F32)
    wc = jnp.concatenate([jnp.kron(eye, cc), jnp.kron(eye, sc)], axis=1).astype(BF16)
    return wc, m


def _conv_fourier(ua, ub, conv_w, conv_p, wc, m, layer, rows_per_batch):
    rows = ua.shape[0]
    n = rows_per_batch
    assert n % MIX_PIECE_ROWS == 0
    blk = lambda w: pl.BlockSpec((n, w), lambda b: (b, 0))
    return pl.pallas_call(
        functools.partial(_conv_fourier_kernel, n=n, scale=float((n * HEAD_DIM) ** -0.5)),
        grid=(rows // n,),
        in_specs=[blk(2 * GROUP_W), blk(GROUP_W), _resident(conv_w.shape[1:], (layer,)),
                  _resident(conv_p.shape[1:], (layer,)), _resident(wc.shape), _resident(m.shape)],
        out_specs=[blk(GROUP_W), blk(GROUP_W)],
        out_shape=[jax.ShapeDtypeStruct((rows, GROUP_W), BF16)] * 2,
        scratch_shapes=[pltpu.VMEM((n + 2 * CONV_HALO, GROUP_W), F32)],
        compiler_params=_params(1), name="conv_fourier",
    )(ua, ub, conv_w, conv_p, wc, m)


def _head_mask(rows_per_head, heads=N_HEADS, head0=0):
    shape = (heads * rows_per_head, GROUP_W)
    row_h = lax.broadcasted_iota(jnp.int32, shape, 0) // rows_per_head + head0
    lane_h = lax.broadcasted_iota(jnp.int32, shape, 1) // HEAD_DIM
    return (row_h == lane_h).astype(F32)


def _stack_heads(q, mask_bf16):
    heads = mask_bf16.shape[0] // q.shape[0]
    return jnp.concatenate([q] * heads, axis=0) * mask_bf16


def _unstack_heads(o, rows_per_head, head0=0):
    heads = o.shape[0] // rows_per_head
    assert heads % 2 == 0 and head0 % 2 == 0
    first = _lane_index((rows_per_head, V7X_LANES)) < HEAD_DIM
    tiles = []
    for pair in range(heads // 2):
        lanes = slice((head0 // 2 + pair) * V7X_LANES, (head0 // 2 + pair + 1) * V7X_LANES)
        lo = o[2 * pair * rows_per_head:(2 * pair + 1) * rows_per_head, lanes]
        hi = o[(2 * pair + 1) * rows_per_head:(2 * pair + 2) * rows_per_head, lanes]
        tiles.append(jnp.where(first, lo, hi))
    return tiles[0] if len(tiles) == 1 else jnp.concatenate(tiles, axis=1)


def _dot_nt(a, b):
    return lax.dot_general(a, b, (((1,), (1,)), ((), ())), preferred_element_type=F32)


def _tree(op, xs):
    while len(xs) > 1:
        xs = [op(xs[i], xs[i + 1]) if i + 1 < len(xs) else xs[i] for i in range(0, len(xs), 2)]
    return xs[0]


def _lane_tiles(*arrays):
    return [a[:, i:i + V7X_LANES] for a in arrays for i in range(0, a.shape[1], V7X_LANES)]


def _ctx_dense(jobs):
    hpc = 2
    chains = [(job, g) for job in jobs for g in range(N_HEADS // hpc)]
    scores = [_dot_nt(_stack_heads(cq_ref[...], _head_mask(CTX_LEN, hpc, g * hpc).astype(BF16)), kx_ref[...])
              for (cq_ref, kx_ref, _, _, _), g in chains]
    probs = []
    for ((_, _, _, sink_ref, _), g), s in zip(chains, scores):
        m = jnp.max(_tree(jnp.maximum, _lane_tiles(s)), axis=-1, keepdims=True)
        sink = None if sink_ref is None else sink_ref[g * hpc * CTX_LEN:(g + 1) * hpc * CTX_LEN, :]
        if sink is not None:
            m = jnp.maximum(m, sink)
        p = jnp.exp2(s - m)
        den = jnp.sum(_tree(jnp.add, _lane_tiles(p)), axis=-1, keepdims=True)
        if sink is not None:
            den = den + jnp.exp2(sink - m)
        probs.append((p.astype(BF16), den))
    outs = [_unstack_heads(jnp.dot(p, vx_ref[...], preferred_element_type=F32) / den, CTX_LEN, g * hpc)
            for ((_, _, vx_ref, _, _), g), (p, den) in zip(chains, probs)]
    for i, (_, _, _, _, oc_ref) in enumerate(jobs):
        oc_ref[...] = jnp.concatenate(outs[i * (N_HEADS // hpc):(i + 1) * (N_HEADS // hpc)], axis=1).astype(BF16)


NAT_ROWS_PER_STEP = 2


def _pipelined_loop(n_steps, scores_fn, softmax_fn, pv_fn):
    assert n_steps % 2 == 0
    scores_fn(0, 0)

    def pair(it, carry):
        for slot in (0, 1):
            step = 2 * it + slot
            scores_fn(step + 1, 1 - slot)
            softmax_fn(slot)
            pv_fn(step, slot)
        return carry

    lax.fori_loop(0, n_steps // 2 - 1, pair, 0)
    scores_fn(n_steps - 1, 1)
    softmax_fn(0)
    pv_fn(n_steps - 2, 0)
    softmax_fn(1)
    pv_fn(n_steps - 1, 1)


def _pipeline_scratch(chains, rows, n_loc):
    kinds = [((chains, rows, n_loc), F32), ((chains, rows, CTX_LEN), F32),
             ((chains, rows, n_loc), BF16), ((chains, rows, CTX_LEN), BF16), ((chains, rows, 1), F32)]
    return tuple(pltpu.VMEM(shape, dtype) for shape, dtype in kinds for _ in range(2))


def _softmax_stage(scratch, slot, chain, sink):
    sl, sc, pl_, pc, dn = (scratch[2 * i + slot] for i in range(5))
    s, c = sl[chain], sc[chain]
    m = jnp.max(_tree(jnp.maximum, _lane_tiles(s, c)), axis=-1, keepdims=True)
    if sink is not None:
        m = jnp.maximum(m, sink)
    p = jnp.exp2(s - m)
    q = jnp.exp2(c - m)
    den = jnp.sum(_tree(jnp.add, _lane_tiles(p, q)), axis=-1, keepdims=True)
    if sink is not None:
        den = den + jnp.exp2(sink - m)
    pl_[chain] = p.astype(BF16)
    pc[chain] = q.astype(BF16)
    dn[chain] = den


def _pv_stage(scratch, slot, chain, vb, vx):
    pl_, pc, dn = (scratch[2 * i + slot] for i in (2, 3, 4))
    o = (jnp.dot(pl_[chain], vb, preferred_element_type=F32) + jnp.dot(pc[chain], vx, preferred_element_type=F32))
    return o / dn[chain]


def _nat_stages(q_ref, k_ref, v_ref, kx_ref, vx_ref, bias_ref, o_ref, scratch):
    mask_b = _head_mask(GRID_W).astype(BF16)
    band = NAT_ROWS * GRID_W
    rps = NAT_ROWS_PER_STEP

    def geometry(r):
        rs = jnp.clip(r - NAT_ROWS // 2, 0, GRID_H - NAT_ROWS)
        return pl.multiple_of(r * GRID_W, GRID_W), pl.multiple_of(rs * GRID_W, GRID_W), rs - r + NAT_ROWS - 1

    def scores(step, slot):
        for j in range(rps):
            q0, k0, d0 = geometry(step * rps + j)
            qst = _stack_heads(q_ref[pl.ds(q0, GRID_W), :], mask_b)
            bias = jnp.concatenate([bias_ref[d0 + 2 * a] for a in range(NAT_ROWS // 2)], axis=1)
            scratch[slot][j] = _dot_nt(qst, k_ref[pl.ds(k0, band), :]) + bias
            scratch[2 + slot][j] = _dot_nt(qst, kx_ref[...])

    def softmax(slot):
        for j in range(rps):
            _softmax_stage(scratch, slot, j, None)

    def pv(step, slot):
        for j in range(rps):
            q0, k0, _ = geometry(step * rps + j)
            o = _pv_stage(scratch, slot, j, v_ref[pl.ds(k0, band), :], vx_ref[...])
            o_ref[pl.ds(q0, GRID_W), :] = _unstack_heads(o, GRID_W).astype(BF16)

    return scores, softmax, pv


def _nat_bias_table(rel_bias):
    depth = rel_bias.shape[0]
    w = GRID_W
    n_dr = 2 * NAT_ROWS - 1
    lo = w - NAT_COLS
    g = jnp.pad(rel_bias.astype(F32) * LOG2_E, ((0, 0),) * 3 + ((lo, 2 * w - lo - (2 * NAT_COLS - 1)),))
    flat = jnp.tile(g, (1, 1, 1, w))[..., :w * (2 * w - 1)]
    t = flat.reshape(depth, N_HEADS, n_dr, w, 2 * w - 1)[..., w - 1:]
    qc = np.arange(w)[:, None]
    kc = np.arange(w)[None, :]
    ws = np.clip(qc - NAT_COLS // 2, 0, w - NAT_COLS)
    col_ok = (kc >= ws) & (kc < ws + NAT_COLS)
    t = jnp.where(col_ok, t, NEG).transpose(0, 2, 1, 3, 4).reshape(depth, n_dr, N_HEADS * w, w)
    return jnp.concatenate([t[:, :-1], t[:, 1:]], axis=-1)


SWA_HEADS_PER_CHAIN = 2


def _swa_stages(q_ref, k_ref, v_ref, kx_ref, vx_ref, sink_ref, win_ref, o_ref, scratch):
    hpc = SWA_HEADS_PER_CHAIN
    chain_rows = hpc * SWA_BLOCK
    band = 3 * SWA_BLOCK
    n_blocks = SEQ // SWA_BLOCK

    @pl.when(pl.program_id(0) == 0)
    def _():
        shape = (chain_rows, band)
        rel = (lax.broadcasted_iota(jnp.int32, shape, 1)
               - lax.broadcasted_iota(jnp.int32, shape, 0) % SWA_BLOCK)
        for i, shift in enumerate((0, -SWA_BLOCK, -2 * SWA_BLOCK)):
            win_ref[i] = jnp.where(jnp.abs(rel + shift) <= SWA_WINDOW, 0.0, NEG)

    masks_b = [_head_mask(SWA_BLOCK, hpc, g * hpc).astype(BF16) for g in range(N_HEADS // hpc)]

    def geometry(nb):
        q0 = pl.multiple_of(nb * SWA_BLOCK, SWA_BLOCK)
        return q0, pl.multiple_of(jnp.clip(q0 - SWA_BLOCK, 0, SEQ - band), SWA_BLOCK)

    def scores(nb, slot):
        q0, k0 = geometry(nb)
        placement = jnp.where(nb == 0, 0, jnp.where(nb == n_blocks - 1, 2, 1))
        q = q_ref[pl.ds(q0, SWA_BLOCK), :]
        for g in range(len(masks_b)):
            qst = _stack_heads(q, masks_b[g])
            scratch[slot][g] = _dot_nt(qst, k_ref[pl.ds(k0, band), :]) + win_ref[placement]
            scratch[2 + slot][g] = _dot_nt(qst, kx_ref[...])

    def softmax(slot):
        for g in range(len(masks_b)):
            _softmax_stage(scratch, slot, g, sink_ref[g * chain_rows:(g + 1) * chain_rows, :])

    def pv(nb, slot):
        q0, k0 = geometry(nb)
        outs = []
        for g in range(len(masks_b)):
            o = _pv_stage(scratch, slot, g, v_ref[pl.ds(k0, band), :], vx_ref[...])
            outs.append(_unstack_heads(o, SWA_BLOCK, g * hpc))
        o_ref[pl.ds(q0, SWA_BLOCK), :] = jnp.concatenate(outs, axis=1).astype(BF16)

    return scores, softmax, pv


ATTENTION_STEPS = GRID_H // NAT_ROWS_PER_STEP
assert ATTENTION_STEPS == SEQ // SWA_BLOCK


def _attention_kernel(*refs, ctx_out):
    refs = list(refs)
    take = lambda n: [refs.pop(0) for _ in range(n)]
    nq, nk, nv, nkx, nvx, sq, sk, sv, skx, svx, bias_ref, sink_ref, csink_ref = take(13)
    cqn, cqs = take(2) if ctx_out else (None, None)
    o_nat, o_swa = take(2)
    oc_nat, oc_swa = take(2) if ctx_out else (None, None)
    (win_ref,), nat_scratch, swa_scratch = take(1), take(10), take(10)
    mixers = [_nat_stages(nq, nk, nv, nkx, nvx, bias_ref, o_nat, nat_scratch),
              _swa_stages(sq, sk, sv, skx, svx, sink_ref, win_ref, o_swa, swa_scratch)]

    def stage(i):
        return lambda *args: [mixer[i](*args) for mixer in mixers]

    _pipelined_loop(ATTENTION_STEPS, stage(0), stage(1), stage(2))
    if ctx_out:
        _ctx_dense([(cqn, nkx, nvx, None, oc_nat), (cqs, skx, svx, csink_ref, oc_swa)])


def _attention(nat_lat, nat_ctx, swa_lat, swa_ctx, nat_bias, sink_lat, sink_ctx, layer, ctx_out):
    batch = nat_lat[0].shape[0] // SEQ
    lat_blk = pl.BlockSpec((SEQ, GROUP_W), lambda b: (b, 0))
    ctx_blk = pl.BlockSpec((CTX_LEN, GROUP_W), lambda b: (b, 0))
    consts = (nat_bias, sink_lat, sink_ctx)
    in_specs = ([lat_blk] * 3 + [ctx_blk] * 2) * 2 + [_resident(c.shape[1:], (layer,)) for c in consts]
    args = list(nat_lat) + list(nat_ctx[1:]) + list(swa_lat) + list(swa_ctx[1:]) + list(consts)
    out_shape = [jax.ShapeDtypeStruct((batch * SEQ, GROUP_W), BF16)] * 2
    out_specs = [lat_blk] * 2
    if ctx_out:
        in_specs += [ctx_blk] * 2
        args += [nat_ctx[0], swa_ctx[0]]
        out_shape += [jax.ShapeDtypeStruct((batch * CTX_LEN, GROUP_W), BF16)] * 2
        out_specs += [ctx_blk] * 2
    swa_rows, swa_chains = SWA_HEADS_PER_CHAIN * SWA_BLOCK, N_HEADS // SWA_HEADS_PER_CHAIN
    scratch = ((pltpu.VMEM((3, swa_rows, 3 * SWA_BLOCK), F32),)
               + _pipeline_scratch(NAT_ROWS_PER_STEP, N_HEADS * GRID_W, NAT_ROWS * GRID_W)
               + _pipeline_scratch(swa_chains, swa_rows, 3 * SWA_BLOCK))
    outs = pl.pallas_call(
        functools.partial(_attention_kernel, ctx_out=ctx_out), grid=(batch,), in_specs=in_specs,
        out_specs=out_specs, out_shape=out_shape, scratch_shapes=list(scratch),
        compiler_params=_params(1), name="attention",
    )(*args)
    return tuple(outs) if ctx_out else (outs[0], outs[1], None, None)


def _rope_tables():
    t = np.arange(SEQ)
    pos = np.stack([t // GRID_W, t % GRID_W], axis=1).astype(np.float32)
    half = HEAD_DIM // 4
    inv = ROPE_BASE ** (-jnp.arange(half, dtype=F32) / half)
    ang = jnp.asarray(pos)[:, :, None] * inv[None, None, :]
    cos = jnp.cos(ang)
    sin = jnp.sin(ang)
    cos64 = jnp.concatenate([cos, cos], axis=-1).reshape(SEQ, HEAD_DIM)
    sin64 = jnp.concatenate([-sin, sin], axis=-1).reshape(SEQ, HEAD_DIM)
    return jnp.tile(cos64, (1, V7X_LANES // HEAD_DIM)), jnp.tile(sin64, (1, V7X_LANES // HEAD_DIM))


def kernel(x, c, ctx, c_ctx, w_ada, b_ada, norm_g, ffn_w1, ffn_w3, ffn_w2, w_in, conv_w, conv_b, conv_ln_g,
           conv_ln_b, nat_rel_bias, sink_logits, w_out):
    bsz, seq, d = x.shape
    depth = w_ada.shape[0]
    assert (seq, d, ctx.shape[1]) == (SEQ, D_MODEL, CTX_LEN) and bsz < MOD_ROWS
    xl = x.reshape(bsz * seq, d)
    xc = ctx.reshape(bsz * CTX_LEN, d)

    cv = jnp.zeros((MOD_ROWS, d), F32).at[:bsz].set(c).at[bsz].set(c_ctx)
    mod = _ada_mod(cv, w_ada, b_ada).reshape(depth, MOD_ROWS, N_MOD, d)
    rope_tabs = _rope_tables()
    dft_lat = _dft_tables(SEQ)
    dft_ctx = _dft_tables(CTX_LEN)
    w1, w3, w2, wx, wo = (w.astype(BF16) for w in (ffn_w1, ffn_w3, ffn_w2, w_in, w_out))
    conv_p = jnp.stack([conv_b, conv_ln_g, conv_ln_b], axis=1)
    nat_bias = _nat_bias_table(nat_rel_bias)
    sink_lat = jnp.repeat(sink_logits * LOG2_E, SWA_BLOCK, axis=1)[..., None]
    sink_ctx = jnp.repeat(sink_logits * LOG2_E, CTX_LEN, axis=1)[..., None]

    for l in range(depth):
        last = l == depth - 1
        xl1, ua, ub, qn, kn, vn, qs, ks, vs = _front(xl, mod, 0, SEQ, l, norm_g, w1, w3, w2, wx, rope_tabs)
        xc1, ca, cb, cqn, ckn, cvn, cqs, cks, cvs = _front(xc, mod, bsz, bsz * CTX_LEN, l, norm_g, w1, w3, w2,
                                                           wx, None)
        y_a, y_b = _conv_fourier(ua, ub, conv_w, conv_p, *dft_lat, l, SEQ)
        y_c, y_d, yc_c, yc_d = _attention((qn, kn, vn), (cqn, ckn, cvn), (qs, ks, vs), (cqs, cks, cvs),
                                          nat_bias, sink_lat, sink_ctx, l, not last)
        xl = _back(xl1, (y_a, y_b, y_c, y_d), mod, 0, SEQ, l, norm_g, wo, w1, w3, w2)
        if not last:
            yc_a, yc_b = _conv_fourier(ca, cb, conv_w, conv_p, *dft_ctx, l, CTX_LEN)
            xc = _back(xc1, (yc_a, yc_b, yc_c, yc_d), mod, bsz, bsz * CTX_LEN, l, norm_g, wo, w1, w3, w2)
    return xl.reshape(bsz, seq, d)
```
